```python
import math
import jax, jax.numpy as jnp
from jax import lax
import numpy as np


D_MODEL = 1024
BATCH = 8
SEQ = 4096
DEPTH = 2

N_A = DEPTH // 2
N_B = DEPTH - N_A

DIFF_HEADS = 8
DIFF_HEAD_DIM = 64
DIFF_V_DIM = 2 * DIFF_HEAD_DIM
DIFF_QK_WIDTH = DIFF_HEADS * 2 * DIFF_HEAD_DIM
DIFF_V_WIDTH = DIFF_HEADS * DIFF_V_DIM

MLA_HEADS = 8
MLA_NOPE = 128
MLA_ROPE = 64
MLA_V = 128
Q_LORA = 384
KV_LORA = 256

D_FF = 4 * D_MODEL

ROPE_THETA = 10000.0
EPS = 1e-6
SUBLN_EPS = 1e-5
Q_BLOCK = 128

kernel_name = "yoco_diffattn_mla_hybrid"


def rms_norm(x, g, eps=EPS):
    xf = x.astype(jnp.float32)
    y = xf * lax.rsqrt(jnp.mean(xf * xf, axis=-1, keepdims=True) + eps)
    return (y * g.astype(jnp.float32)).astype(x.dtype)


def rope_tables(seq, dim):
    pos = jnp.arange(seq, dtype=jnp.float32)
    inv_freq = ROPE_THETA ** (-jnp.arange(0, dim, 2, dtype=jnp.float32) / dim)
    ang = pos[:, None] * inv_freq[None, :]
    return jnp.cos(ang), jnp.sin(ang)


def apply_rope(x, cos, sin):
    half = x.shape[-1] // 2
    x1, x2 = x[..., :half], x[..., half:]
    c = cos[None, :, None, :].astype(x.dtype)
    s = sin[None, :, None, :].astype(x.dtype)
    return jnp.concatenate([x1 * c - x2 * s, x2 * c + x1 * s], axis=-1)


def causal_mask(i, seq):
    q_pos = i * Q_BLOCK + jnp.arange(Q_BLOCK)
    k_pos = jnp.arange(seq)
    return k_pos[None, :] <= q_pos[:, None]


def diff_attention(hn, w_qkv, w_o, lq1, lk1, lq2, lk2, subln_g, lambda_init, cos, sin):
    B, S, _ = hn.shape
    H, d = DIFF_HEADS, DIFF_HEAD_DIM
    qkv = hn @ w_qkv
    q, k, v = jnp.split(qkv, [DIFF_QK_WIDTH, 2 * DIFF_QK_WIDTH], axis=-1)
    q = apply_rope(q.reshape(B, S, 2 * H, d), cos, sin).reshape(B, S, H, 2, d)
    k = apply_rope(k.reshape(B, S, 2 * H, d), cos, sin).reshape(B, S, H, 2, d)
    v = v.reshape(B, S, H, DIFF_V_DIM)
    lam = (jnp.exp(jnp.sum(lq1.astype(jnp.float32) * lk1.astype(jnp.float32)))
           - jnp.exp(jnp.sum(lq2.astype(jnp.float32) * lk2.astype(jnp.float32)))
           + lambda_init)
    scale = d ** -0.5
    nb = S // Q_BLOCK
    qb = q.reshape(B, nb, Q_BLOCK, H, 2, d).transpose(1, 0, 2, 3, 4, 5)

    def block(args):
        q_blk, i = args
        s = jnp.einsum('bqhcd,bkhcd->bhcqk', q_blk, k).astype(jnp.float32) * scale
        s = jnp.where(causal_mask(i, S)[None, None, None], s, -jnp.inf)
        p = jax.nn.softmax(s, axis=-1)
        a = p[:, :, 0] - lam * p[:, :, 1]
        return jnp.einsum('bhqk,bkhe->bqhe', a.astype(v.dtype), v)

    o = lax.map(block, (qb, jnp.arange(nb)))
    o = o.transpose(1, 0, 2, 3, 4).reshape(B, S, H, DIFF_V_DIM)
    o = rms_norm(o, subln_g, SUBLN_EPS) * (1.0 - lambda_init)
    return o.reshape(B, S, DIFF_V_WIDTH) @ w_o


def mla_shared_kv(h, kv_in_norm_g, w_dkv, kv_norm_g, w_ukv, cos, sin):
    B, S, _ = h.shape
    hn = rms_norm(h, kv_in_norm_g)
    ckv = hn @ w_dkv
    c, k_rope = ckv[..., :KV_LORA], ckv[..., KV_LORA:]
    c = rms_norm(c, kv_norm_g)
    kv = (c @ w_ukv).reshape(B, S, MLA_HEADS, MLA_NOPE + MLA_V)
    k_nope, v = kv[..., :MLA_NOPE], kv[..., MLA_NOPE:]
    k_rope = apply_rope(k_rope[:, :, None, :], cos, sin)[:, :, 0, :]
    return k_nope, k_rope, v


def mla_attention(hn, w_dq, q_norm_g, w_uq, w_o, k_nope, k_rope, v, cos, sin):
    B, S, _ = hn.shape
    H = MLA_HEADS
    cq = rms_norm(hn @ w_dq, q_norm_g)
    q = (cq @ w_uq).reshape(B, S, H, MLA_NOPE + MLA_ROPE)
    q_nope = q[..., :MLA_NOPE]
    q_rope = apply_rope(q[..., MLA_NOPE:], cos, sin)
    scale = (MLA_NOPE + MLA_ROPE) ** -0.5
    nb = S // Q_BLOCK
    qn_b = q_nope.reshape(B, nb, Q_BLOCK, H, MLA_NOPE).transpose(1, 0, 2, 3, 4)
    qr_b = q_rope.reshape(B, nb, Q_BLOCK, H, MLA_ROPE).transpose(1, 0, 2, 3, 4)

    def block(args):
        qn, qr, i = args
        s = (jnp.einsum('bqhd,bkhd->bhqk', qn, k_nope)
             + jnp.einsum('bqhr,bkr->bhqk', qr, k_rope)).astype(jnp.float32) * scale
        s = jnp.where(causal_mask(i, S)[None, None], s, -jnp.inf)
        p = jax.nn.softmax(s, axis=-1)
        return jnp.einsum('bhqk,bkhe->bqhe', p.astype(v.dtype), v)

    o = lax.map(block, (qn_b, qr_b, jnp.arange(nb)))
    o = o.transpose(1, 0, 2, 3, 4).reshape(B, S, H * MLA_V)
    return o @ w_o


def sq_relu_mlp(hn, w_up, w_down):
    return jnp.square(jax.nn.relu(hn @ w_up)) @ w_down


def _w(key, shape, fan_in):
    return jax.random.normal(key, shape, jnp.float32) * fan_in ** -0.5


def _g(key, shape):
    return 1.0 + 0.02 * jax.random.normal(key, shape, jnp.float32)


def setup_inputs(seed: int = 0) -> dict:
    key = jax.random.key(seed)
    ks = jax.random.split(key, 24)
    return {
        "x": jax.random.normal(ks[0], (BATCH, SEQ, D_MODEL), jnp.float32),
        "attn_norm_g": _g(ks[1], (DEPTH, D_MODEL)),
        "w_qkv_a": _w(ks[2], (N_A, D_MODEL, 2 * DIFF_QK_WIDTH + DIFF_V_WIDTH), D_MODEL),
        "lambda_q1": 0.1 * jax.random.normal(ks[3], (N_A, DIFF_HEAD_DIM), jnp.float32),
        "lambda_k1": 0.1 * jax.random.normal(ks[4], (N_A, DIFF_HEAD_DIM), jnp.float32),
        "lambda_q2": 0.1 * jax.random.normal(ks[5], (N_A, DIFF_HEAD_DIM), jnp.float32),
        "lambda_k2": 0.1 * jax.random.normal(ks[6], (N_A, DIFF_HEAD_DIM), jnp.float32),
        "subln_g": _g(ks[7], (N_A, DIFF_V_DIM)),
        "w_o_a": _w(ks[8], (N_A, DIFF_V_WIDTH, D_MODEL), DIFF_V_WIDTH),
        "kv_in_norm_g": _g(ks[9], (D_MODEL,)),
        "w_dkv": _w(ks[10], (D_MODEL, KV_LORA + MLA_ROPE), D_MODEL),
        "kv_norm_g": _g(ks[11], (KV_LORA,)),
        "w_ukv": _w(ks[12], (KV_LORA, MLA_HEADS * (MLA_NOPE + MLA_V)), KV_LORA),
        "w_dq": _w(ks[13], (N_B, D_MODEL, Q_LORA), D_MODEL),
        "q_norm_g": _g(ks[14], (N_B, Q_LORA)),
        "w_uq": _w(ks[15], (N_B, Q_LORA, MLA_HEADS * (MLA_NOPE + MLA_ROPE)), Q_LORA),
        "w_o_b": _w(ks[16], (N_B, MLA_HEADS * MLA_V, D_MODEL), MLA_HEADS * MLA_V),
        "mlp_norm_g": _g(ks[17], (DEPTH, D_MODEL)),
        "w_up": _w(ks[18], (DEPTH, D_MODEL, D_FF), D_MODEL),
        "w_down": _w(ks[19], (DEPTH, D_FF, D_MODEL), D_FF),
        "final_norm_g": _g(ks[20], (D_MODEL,)),
    }


def reference(x, attn_norm_g, w_qkv_a, lambda_q1, lambda_k1, lambda_q2, lambda_k2, subln_g, w_o_a,
              kv_in_norm_g, w_dkv, kv_norm_g, w_ukv, w_dq, q_norm_g, w_uq, w_o_b,
              mlp_norm_g, w_up, w_down, final_norm_g):
    S = x.shape[1]
    cos_a, sin_a = rope_tables(S, DIFF_HEAD_DIM)
    cos_b, sin_b = rope_tables(S, MLA_ROPE)
    h = x
    k_nope = k_rope = v = None
    for l in range(DEPTH):
        if l < N_A:
            lambda_init = 0.8 - 0.6 * math.exp(-0.3 * l)
            hn = rms_norm(h, attn_norm_g[l])
            h = h + diff_attention(hn, w_qkv_a[l], w_o_a[l], lambda_q1[l], lambda_k1[l],
                                   lambda_q2[l], lambda_k2[l], subln_g[l], lambda_init,
                                   cos_a, sin_a)
        else:
            if l == N_A:
                k_nope, k_rope, v = mla_shared_kv(h, kv_in_norm_g, w_dkv, kv_norm_g, w_ukv,
                                                  cos_b, sin_b)
            j = l - N_A
            hn = rms_norm(h, attn_norm_g[l])
            h = h + mla_attention(hn, w_dq[j], q_norm_g[j], w_uq[j], w_o_b[j],
                                  k_nope, k_rope, v, cos_b, sin_b)
        h = h + sq_relu_mlp(rms_norm(h, mlp_norm_g[l]), w_up[l], w_down[l])
    return rms_norm(h, final_norm_g)
```

```python
import functools
import math

import jax
import jax.numpy as jnp
from jax import lax
from jax.experimental import pallas as pl
from jax.experimental.pallas import tpu as pltpu

D_MODEL = 1024
DEPTH = 2
N_A = DEPTH // 2
HEADS = 8
HEAD_DIM = 64
V_DIM = 128
MLA_NOPE = 128
MLA_ROPE = 64
Q_LORA = 384
KV_LORA = 256
D_FF = 4 * D_MODEL
ROPE_THETA = 10000.0
EPS = 1e-6
SUBLN_EPS = 1e-5

LANES = 128
SLOT = 2 * LANES
TOKEN_TILE = 512
ATTN_TILE = 256
FF_CHUNK = 1024
NEG_BIG = -1e30
VMEM_LIMIT = 56 * 1024 * 1024

BF16 = jnp.bfloat16
F32 = jnp.float32
NT_DIMS = (((1,), (1,)), ((), ()))


def _rms(x, g, eps):
    return x * lax.rsqrt(jnp.mean(x * x, axis=-1, keepdims=True) + eps) * g


def _rope_slab(x, cos, sin_signed):
    lane = lax.broadcasted_iota(jnp.int32, x.shape, 1)
    first = (lane % HEAD_DIM) < (HEAD_DIM // 2)
    swapped = jnp.where(first, pltpu.roll(x, LANES - HEAD_DIM // 2, 1), pltpu.roll(x, HEAD_DIM // 2, 1))
    return x * cos + swapped * sin_signed


def _dot(a, b):
    return jnp.dot(a, b, preferred_element_type=F32)


def _proj_a_kernel(x_ref, g_ref, w_ref, cos_ref, sin_ref, qz_ref, k_ref, v_ref):
    hn = _rms(x_ref[...], g_ref[...], EPS).astype(BF16)
    qkv = _dot(hn, w_ref[...])
    cos, sin = cos_ref[...], sin_ref[...]
    lane = lax.broadcasted_iota(jnp.int32, (x_ref.shape[0], LANES), 1)
    comp1 = lane < HEAD_DIM
    scale = HEAD_DIM ** -0.5
    for h in range(HEADS):
        q = _rope_slab(qkv[:, h * LANES:(h + 1) * LANES], cos, sin) * scale
        qz_ref[:, h * SLOT:h * SLOT + LANES] = jnp.where(comp1, q, 0.0).astype(BF16)
        qz_ref[:, h * SLOT + LANES:(h + 1) * SLOT] = jnp.where(comp1, 0.0, q).astype(BF16)
        k = _rope_slab(qkv[:, D_MODEL + h * LANES:D_MODEL + (h + 1) * LANES], cos, sin)
        k_ref[:, h * LANES:(h + 1) * LANES] = k.astype(BF16)
    v_ref[...] = qkv[:, 2 * D_MODEL:].astype(BF16)


def _proj_a(x2, g, w_qkv, cos, sin, seq):
    t = x2.shape[0]
    tm = TOKEN_TILE
    pos_blocks = seq // tm
    const = lambda i: (0, 0)
    return pl.pallas_call(
        _proj_a_kernel,
        grid=(t // tm,),
        in_specs=[
            pl.BlockSpec((tm, D_MODEL), lambda i: (i, 0)),
            pl.BlockSpec((1, D_MODEL), const),
            pl.BlockSpec((D_MODEL, 3 * D_MODEL), const),
            pl.BlockSpec((tm, LANES), lambda i: (i % pos_blocks, 0)),
            pl.BlockSpec((tm, LANES), lambda i: (i % pos_blocks, 0)),
        ],
        out_specs=[
            pl.BlockSpec((tm, HEADS * SLOT), lambda i: (i, 0)),
            pl.BlockSpec((tm, D_MODEL), lambda i: (i, 0)),
            pl.BlockSpec((tm, D_MODEL), lambda i: (i, 0)),
        ],
        out_shape=[
            jax.ShapeDtypeStruct((t, HEADS * SLOT), BF16),
            jax.ShapeDtypeStruct((t, D_MODEL), BF16),
            jax.ShapeDtypeStruct((t, D_MODEL), BF16),
        ],
        compiler_params=pltpu.CompilerParams(dimension_semantics=("arbitrary",), vmem_limit_bytes=VMEM_LIMIT),
        name="proj_a",
    )(x2, g, w_qkv, cos, sin)


def _flash_update(s_t, vt_t, m_ref, l_ref, acc_ref, c):
    m_old = m_ref[c]
    m_new = jnp.maximum(m_old, jnp.max(s_t, axis=0, keepdims=True))
    alpha = jnp.exp(m_old - m_new)
    p = jnp.exp(s_t - m_new)
    l_ref[c] = alpha * l_ref[c] + jnp.sum(p, axis=0, keepdims=True)
    acc_ref[c] = alpha * acc_ref[c] + _dot(vt_t, p.astype(BF16))
    m_ref[c] = m_new


def _causal_keep(tile):
    key = lax.broadcasted_iota(jnp.int32, (tile, tile), 0)
    qry = lax.broadcasted_iota(jnp.int32, (tile, tile), 1)
    return key <= qry


def _flash_q_tile(q_tiles, k_ref, vt_ref, m_ref, l_ref, acc_ref, qi):
    t = ATTN_TILE
    n = len(q_tiles)
    m_ref[...] = jnp.full(m_ref.shape, NEG_BIG, F32)
    l_ref[...] = jnp.zeros(l_ref.shape, F32)
    acc_ref[...] = jnp.zeros(acc_ref.shape, F32)

    def step(kj, masked):
        k_t = k_ref[0, pl.ds(pl.multiple_of(kj * t, t), t), :]
        vt_t = vt_ref[0, 0, kj]
        for c in range(n):
            s_t = lax.dot_general(k_t, q_tiles[c], NT_DIMS, preferred_element_type=F32)
            if masked:
                s_t = jnp.where(_causal_keep(t), s_t, NEG_BIG)
            _flash_update(s_t, vt_t, m_ref, l_ref, acc_ref, c)

    def body(kj, carry):
        step(kj, False)
        return carry

    lax.fori_loop(0, qi, body, 0)
    step(qi, True)


def _diff_attn_kernel(qz_ref, k_ref, vt_ref, g_ref, lq1_ref, lk1_ref, lq2_ref, lk2_ref, o_ref,
                      m_ref, l_ref, acc_ref, *, lambda_init):
    t = ATTN_TILE
    lam = (jnp.exp(jnp.sum(lq1_ref[...] * lk1_ref[...], keepdims=True))
           - jnp.exp(jnp.sum(lq2_ref[...] * lk2_ref[...], keepdims=True)) + lambda_init)
    g_col = g_ref[...]

    def q_body(qi, carry):
        rows = pl.ds(pl.multiple_of(qi * t, t), t)
        q_tiles = [qz_ref[0, rows, 0:LANES], qz_ref[0, rows, LANES:SLOT]]
        _flash_q_tile(q_tiles, k_ref, vt_ref, m_ref, l_ref, acc_ref, qi)
        o_t = acc_ref[0] * (1.0 / l_ref[0]) - lam * (acc_ref[1] * (1.0 / l_ref[1]))
        y_t = o_t * lax.rsqrt(jnp.mean(o_t * o_t, axis=0, keepdims=True) + SUBLN_EPS) * g_col
        o_ref[0, rows, :] = (y_t * (1.0 - lambda_init)).T.astype(BF16)
        return carry

    lax.fori_loop(0, o_ref.shape[1] // t, q_body, 0)


def _mla_attn_kernel(q_ref, k_ref, vt_ref, o_ref, m_ref, l_ref, acc_ref):
    t = ATTN_TILE

    def q_body(qi, carry):
        rows = pl.ds(pl.multiple_of(qi * t, t), t)
        _flash_q_tile([q_ref[0, rows, :]], k_ref, vt_ref, m_ref, l_ref, acc_ref, qi)
        o_t = acc_ref[0] * (1.0 / l_ref[0])
        o_ref[0, rows, :] = o_t.T.astype(BF16)
        return carry

    lax.fori_loop(0, o_ref.shape[1] // t, q_body, 0)


def _attn_call(kernel, n_softmax, q, k, k_width, vt, extra, name):
    b, s, _ = q.shape
    t = ATTN_TILE
    small = lambda bi, hi: (0, 0)
    in_specs = [
        pl.BlockSpec((1, s, SLOT), lambda bi, hi: (bi, 0, hi)),
        pl.BlockSpec((1, s, k_width), lambda bi, hi: (bi, 0, hi)),
        pl.BlockSpec((1, 1, s // t, V_DIM, t), lambda bi, hi: (bi, hi, 0, 0, 0)),
    ] + [pl.BlockSpec(e.shape, small) for e in extra]
    return pl.pallas_call(
        kernel,
        grid=(b, HEADS),
        in_specs=in_specs,
        out_specs=pl.BlockSpec((1, s, V_DIM), lambda bi, hi: (bi, 0, hi)),
        out_shape=jax.ShapeDtypeStruct((b, s, HEADS * V_DIM), BF16),
        scratch_shapes=[
            pltpu.VMEM((n_softmax, 1, t), F32),
            pltpu.VMEM((n_softmax, 1, t), F32),
            pltpu.VMEM((n_softmax, V_DIM, t), F32),
        ],
        compiler_params=pltpu.CompilerParams(dimension_semantics=("arbitrary", "arbitrary"),
                                             vmem_limit_bytes=VMEM_LIMIT),
        name=name,
    )(q, k, vt, *extra)


def _values_transposed(v2, b, s):
    t = ATTN_TILE
    return v2.reshape(b, s // t, t, HEADS, V_DIM).transpose(0, 3, 1, 4, 2)


def _post_kernel(h_ref, o_ref, wo_ref, g_ref, wup_ref, wdn_ref, gf_ref, out_ref, *, final_norm):
    h1 = h_ref[...] + _dot(o_ref[...], wo_ref[...])
    hn = _rms(h1, g_ref[...], EPS).astype(BF16)
    acc = h1
    for f in range(D_FF // FF_CHUNK):
        cols = slice(f * FF_CHUNK, (f + 1) * FF_CHUNK)
        u = jnp.maximum(_dot(hn, wup_ref[:, cols]), 0.0)
        acc = acc + _dot((u * u).astype(BF16), wdn_ref[cols, :])
    if final_norm:
        acc = _rms(acc, gf_ref[...], EPS)
    out_ref[...] = acc


def _post(h2, o2, w_o, g_mlp, w_up, w_down, g_final, final_norm, name):
    t = h2.shape[0]
    tm = TOKEN_TILE
    const = lambda i: (0, 0)
    row = lambda i: (i, 0)
    resident = functools.partial(pl.BlockSpec, index_map=const, pipeline_mode=pl.Buffered(1))
    return pl.pallas_call(
        functools.partial(_post_kernel, final_norm=final_norm),
        grid=(t // tm,),
        in_specs=[
            pl.BlockSpec((tm, D_MODEL), row),
            pl.BlockSpec((tm, D_MODEL), row),
            resident((D_MODEL, D_MODEL)),
            pl.BlockSpec((1, D_MODEL), const),
            resident((D_MODEL, D_FF)),
            resident((D_FF, D_MODEL)),
            pl.BlockSpec((1, D_MODEL), const),
        ],
        out_specs=pl.BlockSpec((tm, D_MODEL), row),
        out_shape=jax.ShapeDtypeStruct((t, D_MODEL), F32),
        compiler_params=pltpu.CompilerParams(dimension_semantics=("arbitrary",), vmem_limit_bytes=VMEM_LIMIT),
        name=name,
    )(h2, o2, w_o, g_mlp, w_up, w_down, g_final)


def _proj_b_kernel(h_ref, gkv_ref, wdkv_ref, gc_ref, wuk_ref, wuv_ref, gq_ref, wdq_ref, gcq_ref, wuq_ref,
                   cos_ref, sin_ref, q_ref, k_ref, v_ref):
    h = h_ref[...]
    cos, sin = cos_ref[...], sin_ref[...]
    ckv = _dot(_rms(h, gkv_ref[...], EPS).astype(BF16), wdkv_ref[...])
    c = _rms(ckv[:, :KV_LORA], gc_ref[...], EPS).astype(BF16)
    k_rope = _rope_slab(ckv[:, KV_LORA:], cos, sin).astype(BF16)
    k_nope = _dot(c, wuk_ref[...]).astype(BF16)
    for hd in range(HEADS):
        k_ref[:, hd * SLOT:hd * SLOT + LANES] = k_nope[:, hd * LANES:(hd + 1) * LANES]
        k_ref[:, hd * SLOT + LANES:(hd + 1) * SLOT] = k_rope
    v_ref[...] = _dot(c, wuv_ref[...]).astype(BF16)
    cq = _rms(_dot(_rms(h, gq_ref[...], EPS).astype(BF16), wdq_ref[...]), gcq_ref[...], EPS).astype(BF16)
    q = _dot(cq, wuq_ref[...])
    scale = (MLA_NOPE + MLA_ROPE) ** -0.5
    for hd in range(HEADS):
        q_ref[:, hd * SLOT:hd * SLOT + LANES] = (q[:, hd * SLOT:hd * SLOT + LANES] * scale).astype(BF16)
        q_rope = _rope_slab(q[:, hd * SLOT + LANES:(hd + 1) * SLOT], cos, sin)
        q_ref[:, hd * SLOT + LANES:(hd + 1) * SLOT] = (q_rope * scale).astype(BF16)


def _proj_b(h2, g_kv_in, w_dkv, g_c, w_uk, w_uv, g_q, w_dq, g_cq, w_uq, cos, sin, seq):
    t = h2.shape[0]
    tm = TOKEN_TILE
    pos_blocks = seq // tm
    const = lambda i: (0, 0)
    row = lambda i: (i, 0)
    full = lambda a: pl.BlockSpec(a.shape, const)
    return pl.pallas_call(
        _proj_b_kernel,
        grid=(t // tm,),
        in_specs=[
            pl.BlockSpec((tm, D_MODEL), row),
            full(g_kv_in), full(w_dkv), full(g_c), full(w_uk), full(w_uv),
            full(g_q), full(w_dq), full(g_cq), full(w_uq),
            pl.BlockSpec((tm, LANES), lambda i: (i % pos_blocks, 0)),
            pl.BlockSpec((tm, LANES), lambda i: (i % pos_blocks, 0)),
        ],
        out_specs=[
            pl.BlockSpec((tm, HEADS * SLOT), row),
            pl.BlockSpec((tm, HEADS * SLOT), row),
            pl.BlockSpec((tm, HEADS * V_DIM), row),
        ],
        out_shape=[
            jax.ShapeDtypeStruct((t, HEADS * SLOT), BF16),
            jax.ShapeDtypeStruct((t, HEADS * SLOT), BF16),
            jax.ShapeDtypeStruct((t, HEADS * V_DIM), BF16),
        ],
        compiler_params=pltpu.CompilerParams(dimension_semantics=("arbitrary",), vmem_limit_bytes=VMEM_LIMIT),
        name="proj_b",
    )(h2, g_kv_in, w_dkv, g_c, w_uk, w_uv, g_q, w_dq, g_cq, w_uq, cos, sin)


def _rope_tables(seq):
    pos = jnp.arange(seq, dtype=F32)
    inv_freq = ROPE_THETA ** (-jnp.arange(0, HEAD_DIM, 2, dtype=F32) / HEAD_DIM)
    ang = pos[:, None] * inv_freq[None, :]
    cos, sin = jnp.cos(ang), jnp.sin(ang)
    return jnp.tile(cos, (1, 4)), jnp.tile(jnp.concatenate([-sin, sin], axis=1), (1, 2))


def _pad_cols(w, width):
    return jnp.pad(w, ((0, 0), (0, width - w.shape[1])))


def kernel(x, attn_norm_g, w_qkv_a, lambda_q1, lambda_k1, lambda_q2, lambda_k2, subln_g, w_o_a, kv_in_norm_g, w_dkv, kv_norm_g, w_ukv, w_dq, q_norm_g, w_uq, w_o_b, mlp_norm_g, w_up, w_down, final_norm_g):
    b, s, _ = x.shape
    t = b * s
    cos, sin = _rope_tables(s)
    row = lambda g: g.reshape(1, -1)
    h = x.reshape(t, D_MODEL)

    lambda_init = 0.8 - 0.6 * math.exp(-0.3 * 0)
    qz, k, v = _proj_a(h, row(attn_norm_g[0]), w_qkv_a[0].astype(BF16), cos, sin, s)
    lam_rows = [row(p[0]) for p in (lambda_q1, lambda_k1, lambda_q2, lambda_k2)]
    o = _attn_call(functools.partial(_diff_attn_kernel, lambda_init=lambda_init), 2,
                   qz.reshape(b, s, HEADS * SLOT), k.reshape(b, s, D_MODEL), LANES,
                   _values_transposed(v, b, s), [subln_g[0].reshape(V_DIM, 1)] + lam_rows, "diff_attn")
    h = _post(h, o.reshape(t, D_MODEL), w_o_a[0].astype(BF16), row(mlp_norm_g[0]),
              w_up[0].astype(BF16), w_down[0].astype(BF16), row(final_norm_g), False, "post_a")

    w_ukv_h = w_ukv.reshape(KV_LORA, HEADS, MLA_NOPE + V_DIM)
    w_uk = w_ukv_h[:, :, :MLA_NOPE].reshape(KV_LORA, HEADS * MLA_NOPE)
    w_uv = w_ukv_h[:, :, MLA_NOPE:].reshape(KV_LORA, HEADS * V_DIM)
    w_uq_slots = jnp.pad(w_uq[0].reshape(Q_LORA, HEADS, MLA_NOPE + MLA_ROPE),
                         ((0, 0), (0, 0), (0, SLOT - MLA_NOPE - MLA_ROPE))).reshape(Q_LORA, HEADS * SLOT)
    q, kk, v = _proj_b(h, row(kv_in_norm_g), _pad_cols(w_dkv, KV_LORA + LANES).astype(BF16), row(kv_norm_g),
                       w_uk.astype(BF16), w_uv.astype(BF16), row(attn_norm_g[1]), w_dq[0].astype(BF16),
                       row(q_norm_g[0]), w_uq_slots.astype(BF16), cos, sin, s)
    o = _attn_call(_mla_attn_kernel, 1, q.reshape(b, s, HEADS * SLOT), kk.reshape(b, s, HEADS * SLOT), SLOT,
                   _values_transposed(v, b, s), [], "mla_attn")
    h = _post(h, o.reshape(t, D_MODEL), w_o_b[0].astype(BF16), row(mlp_norm_g[1]),
              w_up[1].astype(BF16), w_down[1].astype(BF16), row(final_norm_g), True, "post_b")
    return h.reshape(b, s, D_MODEL)
```

```python
import functools
import math

import jax
import jax.numpy as jnp
from jax import lax
from jax.experimental import pallas as pl
from jax.experimental.pallas import tpu as pltpu

D_MODEL = 1024
DEPTH = 2
N_A = DEPTH // 2
HEADS = 8
HEAD_DIM = 64
V_DIM = 128
MLA_NOPE = 128
MLA_ROPE = 64
Q_LORA = 384
KV_LORA = 256
D_FF = 4 * D_MODEL
ROPE_THETA = 10000.0
EPS = 1e-6
SUBLN_EPS = 1e-5

LANES = 128
SLOT = 2 * LANES
TOKEN_TILE = 512
ATTN_TILE = 256
Q_SUB = 4
FF_CHUNK = 1024
NEG_BIG = -1e30
LOG2_E = math.log2(math.e)
VMEM_LIMIT = 56 * 1024 * 1024

BF16 = jnp.bfloat16
F32 = jnp.float32
NT_DIMS = (((1,), (1,)), ((), ()))


def _rms(x, g, eps):
    return x * lax.rsqrt(jnp.mean(x * x, axis=-1, keepdims=True) + eps) * g


def _rope_slab(x, cos, sin_signed):
    lane = lax.broadcasted_iota(jnp.int32, x.shape, 1)
    first = (lane % HEAD_DIM) < (HEAD_DIM // 2)
    swapped = jnp.where(first, pltpu.roll(x, LANES - HEAD_DIM // 2, 1), pltpu.roll(x, HEAD_DIM // 2, 1))
    return x * cos + swapped * sin_signed


def _dot(a, b):
    return jnp.dot(a, b, preferred_element_type=F32)


def _proj_a_kernel(x_ref, g_ref, w_ref, cos_ref, sin_ref, qz_ref, k_ref, v_ref):
    hn = _rms(x_ref[...], g_ref[...], EPS).astype(BF16)
    qkv = _dot(hn, w_ref[...])
    cos, sin = cos_ref[...], sin_ref[...]
    lane = lax.broadcasted_iota(jnp.int32, (x_ref.shape[0], LANES), 1)
    comp1 = lane < HEAD_DIM
    scale = HEAD_DIM ** -0.5 * LOG2_E
    for h in range(HEADS):
        q = _rope_slab(qkv[:, h * LANES:(h + 1) * LANES], cos, sin) * scale
        qz_ref[:, h * SLOT:h * SLOT + LANES] = jnp.where(comp1, q, 0.0).astype(BF16)
        qz_ref[:, h * SLOT + LANES:(h + 1) * SLOT] = jnp.where(comp1, 0.0, q).astype(BF16)
        k = _rope_slab(qkv[:, D_MODEL + h * LANES:D_MODEL + (h + 1) * LANES], cos, sin)
        k_ref[:, h * LANES:(h + 1) * LANES] = k.astype(BF16)
    v_ref[...] = qkv[:, 2 * D_MODEL:].astype(BF16)


def _proj_a(x2, g, w_qkv, cos, sin, seq):
    t = x2.shape[0]
    tm = TOKEN_TILE
    pos_blocks = seq // tm
    const = lambda i: (0, 0)
    return pl.pallas_call(
        _proj_a_kernel,
        grid=(t // tm,),
        in_specs=[
            pl.BlockSpec((tm, D_MODEL), lambda i: (i, 0)),
            pl.BlockSpec((1, D_MODEL), const),
            pl.BlockSpec((D_MODEL, 3 * D_MODEL), const),
            pl.BlockSpec((tm, LANES), lambda i: (i % pos_blocks, 0)),
            pl.BlockSpec((tm, LANES), lambda i: (i % pos_blocks, 0)),
        ],
        out_specs=[
            pl.BlockSpec((tm, HEADS * SLOT), lambda i: (i, 0)),
            pl.BlockSpec((tm, D_MODEL), lambda i: (i, 0)),
            pl.BlockSpec((tm, D_MODEL), lambda i: (i, 0)),
        ],
        out_shape=[
            jax.ShapeDtypeStruct((t, HEADS * SLOT), BF16),
            jax.ShapeDtypeStruct((t, D_MODEL), BF16),
            jax.ShapeDtypeStruct((t, D_MODEL), BF16),
        ],
        compiler_params=pltpu.CompilerParams(dimension_semantics=("arbitrary",), vmem_limit_bytes=VMEM_LIMIT),
        name="proj_a",
    )(x2, g, w_qkv, cos, sin)


def _flash_update(s_t, vt_t, m_ref, l_ref, acc_ref, c):
    m_old = m_ref[c]
    m_new = jnp.maximum(m_old, jnp.max(s_t, axis=0, keepdims=True))
    alpha = jnp.exp2(m_old - m_new)
    p = jnp.exp2(s_t - m_new)
    l_ref[c] = alpha * l_ref[c] + jnp.sum(p, axis=0, keepdims=True)
    acc_ref[c] = alpha * acc_ref[c] + _dot(vt_t, p.astype(BF16))
    m_ref[c] = m_new


def _causal_keep(tile):
    key = lax.broadcasted_iota(jnp.int32, (tile, tile), 0)
    qry = lax.broadcasted_iota(jnp.int32, (tile, tile), 1)
    return key <= qry


def _flash_super_tile(q_ref, q_cols, k_ref, vt_ref, s_ref, m_ref, l_ref, acc_ref, sup):
    t = ATTN_TILE
    n = len(q_cols)
    m_ref[...] = jnp.full(m_ref.shape, NEG_BIG, F32)
    l_ref[...] = jnp.zeros(l_ref.shape, F32)
    acc_ref[...] = jnp.zeros(acc_ref.shape, F32)
    first = sup * Q_SUB
    every = list(range(Q_SUB))

    def scores(kj, subs, slot):
        k_t = k_ref[0, pl.ds(pl.multiple_of(kj * t, t), t), :]
        for a in subs:
            rows = pl.ds(pl.multiple_of((first + a) * t, t), t)
            for c in range(n):
                s_ref[slot, a * n + c] = lax.dot_general(k_t, q_ref[0, rows, q_cols[c]], NT_DIMS,
                                                         preferred_element_type=F32)

    def update(kj, subs, diag, slot):
        vt_t = vt_ref[0, 0, kj]
        for a in subs:
            for c in range(n):
                s_t = s_ref[slot, a * n + c]
                if a == diag:
                    s_t = jnp.where(_causal_keep(t), s_t, NEG_BIG)
                _flash_update(s_t, vt_t, m_ref, l_ref, acc_ref, a * n + c)

    scores(0, every, 0)

    def pair_body(jj, carry):
        kj = 2 * jj
        scores(kj + 1, every, 1)
        update(kj, every, None, 0)
        scores(kj + 2, every, 0)
        update(kj + 1, every, None, 1)
        return carry

    lax.fori_loop(0, sup * (Q_SUB // 2), pair_body, 0)
    for d in range(Q_SUB):
        if d + 1 < Q_SUB:
            scores(first + d + 1, every[d + 1:], (d + 1) % 2)
        update(first + d, every[d:], d, d % 2)


def _diff_attn_kernel(qz_ref, k_ref, vt_ref, g_ref, lq1_ref, lk1_ref, lq2_ref, lk2_ref, o_ref,
                      s_ref, m_ref, l_ref, acc_ref, *, lambda_init):
    t = ATTN_TILE
    lam = (jnp.exp(jnp.sum(lq1_ref[...] * lk1_ref[...], keepdims=True))
           - jnp.exp(jnp.sum(lq2_ref[...] * lk2_ref[...], keepdims=True)) + lambda_init)
    g_col = g_ref[...]

    def sup_body(sup, carry):
        _flash_super_tile(qz_ref, [slice(0, LANES), slice(LANES, SLOT)], k_ref, vt_ref,
                          s_ref, m_ref, l_ref, acc_ref, sup)
        for a in range(Q_SUB):
            c1, c2 = 2 * a, 2 * a + 1
            o_t = acc_ref[c1] * (1.0 / l_ref[c1]) - lam * (acc_ref[c2] * (1.0 / l_ref[c2]))
            y_t = o_t * lax.rsqrt(jnp.mean(o_t * o_t, axis=0, keepdims=True) + SUBLN_EPS) * g_col
            rows = pl.ds(pl.multiple_of((sup * Q_SUB + a) * t, t), t)
            o_ref[0, rows, :] = (y_t * (1.0 - lambda_init)).T.astype(BF16)
        return carry

    lax.fori_loop(0, o_ref.shape[1] // (t * Q_SUB), sup_body, 0)


def _mla_attn_kernel(q_ref, k_ref, vt_ref, o_ref, s_ref, m_ref, l_ref, acc_ref):
    t = ATTN_TILE

    def sup_body(sup, carry):
        _flash_super_tile(q_ref, [slice(0, SLOT)], k_ref, vt_ref, s_ref, m_ref, l_ref, acc_ref, sup)
        for a in range(Q_SUB):
            rows = pl.ds(pl.multiple_of((sup * Q_SUB + a) * t, t), t)
            o_ref[0, rows, :] = (acc_ref[a] * (1.0 / l_ref[a])).T.astype(BF16)
        return carry

    lax.fori_loop(0, o_ref.shape[1] // (t * Q_SUB), sup_body, 0)


def _attn_call(kernel, n_softmax, q, k, k_width, vt, extra, name):
    b, s, _ = q.shape
    t = ATTN_TILE
    chains = n_softmax * Q_SUB
    small = lambda bi, hi: (0, 0)
    in_specs = [
        pl.BlockSpec((1, s, SLOT), lambda bi, hi: (bi, 0, hi)),
        pl.BlockSpec((1, s, k_width), lambda bi, hi: (bi, 0, hi)),
        pl.BlockSpec((1, 1, s // t, V_DIM, t), lambda bi, hi: (bi, hi, 0, 0, 0)),
    ] + [pl.BlockSpec(e.shape, small) for e in extra]
    return pl.pallas_call(
        kernel,
        grid=(b, HEADS),
        in_specs=in_specs,
        out_specs=pl.BlockSpec((1, s, V_DIM), lambda bi, hi: (bi, 0, hi)),
        out_shape=jax.ShapeDtypeStruct((b, s, HEADS * V_DIM), BF16),
        scratch_shapes=[
            pltpu.VMEM((2, chains, t, t), F32),
            pltpu.VMEM((chains, 1, t), F32),
            pltpu.VMEM((chains, 1, t), F32),
            pltpu.VMEM((chains, V_DIM, t), F32),
        ],
        compiler_params=pltpu.CompilerParams(dimension_semantics=("arbitrary", "arbitrary"),
                                             vmem_limit_bytes=VMEM_LIMIT),
        name=name,
    )(q, k, vt, *extra)


def _values_transposed(v2, b, s):
    t = ATTN_TILE
    return v2.reshape(b, s // t, t, HEADS, V_DIM).transpose(0, 3, 1, 4, 2)


def _post_kernel(h_ref, o_ref, wo_ref, g_ref, wup_ref, wdn_ref, gf_ref, out_ref, *, final_norm):
    h1 = h_ref[...] + _dot(o_ref[...], wo_ref[...])
    hn = _rms(h1, g_ref[...], EPS).astype(BF16)
    acc = h1
    for f in range(D_FF // FF_CHUNK):
        cols = slice(f * FF_CHUNK, (f + 1) * FF_CHUNK)
        u = jnp.maximum(_dot(hn, wup_ref[:, cols]), 0.0)
        acc = acc + _dot((u * u).astype(BF16), wdn_ref[cols, :])
    if final_norm:
        acc = _rms(acc, gf_ref[...], EPS)
    out_ref[...] = acc


def _post(h2, o2, w_o, g_mlp, w_up, w_down, g_final, final_norm, name):
    t = h2.shape[0]
    tm = TOKEN_TILE
    const = lambda i: (0, 0)
    row = lambda i: (i, 0)
    resident = functools.partial(pl.BlockSpec, index_map=const, pipeline_mode=pl.Buffered(1))
    return pl.pallas_call(
        functools.partial(_post_kernel, final_norm=final_norm),
        grid=(t // tm,),
        in_specs=[
            pl.BlockSpec((tm, D_MODEL), row),
            pl.BlockSpec((tm, D_MODEL), row),
            resident((D_MODEL, D_MODEL)),
            pl.BlockSpec((1, D_MODEL), const),
            resident((D_MODEL, D_FF)),
            resident((D_FF, D_MODEL)),
            pl.BlockSpec((1, D_MODEL), const),
        ],
        out_specs=pl.BlockSpec((tm, D_MODEL), row),
        out_shape=jax.ShapeDtypeStruct((t, D_MODEL), F32),
        compiler_params=pltpu.CompilerParams(dimension_semantics=("arbitrary",), vmem_limit_bytes=VMEM_LIMIT),
        name=name,
    )(h2, o2, w_o, g_mlp, w_up, w_down, g_final)


def _proj_b_kernel(h_ref, gkv_ref, wdkv_ref, gc_ref, wuk_ref, wuv_ref, gq_ref, wdq_ref, gcq_ref, wuq_ref,
                   cos_ref, sin_ref, q_ref, k_ref, v_ref):
    h = h_ref[...]
    cos, sin = cos_ref[...], sin_ref[...]
    ckv = _dot(_rms(h, gkv_ref[...], EPS).astype(BF16), wdkv_ref[...])
    c = _rms(ckv[:, :KV_LORA], gc_ref[...], EPS).astype(BF16)
    k_rope = _rope_slab(ckv[:, KV_LORA:], cos, sin).astype(BF16)
    k_nope = _dot(c, wuk_ref[...]).astype(BF16)
    for hd in range(HEADS):
        k_ref[:, hd * SLOT:hd * SLOT + LANES] = k_nope[:, hd * LANES:(hd + 1) * LANES]
        k_ref[:, hd * SLOT + LANES:(hd + 1) * SLOT] = k_rope
    v_ref[...] = _dot(c, wuv_ref[...]).astype(BF16)
    cq = _rms(_dot(_rms(h, gq_ref[...], EPS).astype(BF16), wdq_ref[...]), gcq_ref[...], EPS).astype(BF16)
    q = _dot(cq, wuq_ref[...])
    scale = (MLA_NOPE + MLA_ROPE) ** -0.5 * LOG2_E
    for hd in range(HEADS):
        q_ref[:, hd * SLOT:hd * SLOT + LANES] = (q[:, hd * SLOT:hd * SLOT + LANES] * scale).astype(BF16)
        q_rope = _rope_slab(q[:, hd * SLOT + LANES:(hd + 1) * SLOT], cos, sin)
        q_ref[:, hd * SLOT + LANES:(hd + 1) * SLOT] = (q_rope * scale).astype(BF16)


def _proj_b(h2, g_kv_in, w_dkv, g_c, w_uk, w_uv, g_q, w_dq, g_cq, w_uq, cos, sin, seq):
    t = h2.shape[0]
    tm = TOKEN_TILE
    pos_blocks = seq // tm
    const = lambda i: (0, 0)
    row = lambda i: (i, 0)
    full = lambda a: pl.BlockSpec(a.shape, const)
    return pl.pallas_call(
        _proj_b_kernel,
        grid=(t // tm,),
        in_specs=[
            pl.BlockSpec((tm, D_MODEL), row),
            full(g_kv_in), full(w_dkv), full(g_c), full(w_uk), full(w_uv),
            full(g_q), full(w_dq), full(g_cq), full(w_uq),
            pl.BlockSpec((tm, LANES), lambda i: (i % pos_blocks, 0)),
            pl.BlockSpec((tm, LANES), lambda i: (i % pos_blocks, 0)),
        ],
        out_specs=[
            pl.BlockSpec((tm, HEADS * SLOT), row),
            pl.BlockSpec((tm, HEADS * SLOT), row),
            pl.BlockSpec((tm, HEADS * V_DIM), row),
        ],
        out_shape=[
            jax.ShapeDtypeStruct((t, HEADS * SLOT), BF16),
            jax.ShapeDtypeStruct((t, HEADS * SLOT), BF16),
            jax.ShapeDtypeStruct((t, HEADS * V_DIM), BF16),
        ],
        compiler_params=pltpu.CompilerParams(dimension_semantics=("arbitrary",), vmem_limit_bytes=VMEM_LIMIT),
        name="proj_b",
    )(h2, g_kv_in, w_dkv, g_c, w_uk, w_uv, g_q, w_dq, g_cq, w_uq, cos, sin)


def _rope_tables(seq):
    pos = jnp.arange(seq, dtype=F32)
    inv_freq = ROPE_THETA ** (-jnp.arange(0, HEAD_DIM, 2, dtype=F32) / HEAD_DIM)
    ang = pos[:, None] * inv_freq[None, :]
    cos, sin = jnp.cos(ang), jnp.sin(ang)
    return jnp.tile(cos, (1, 4)), jnp.tile(jnp.concatenate([-sin, sin], axis=1), (1, 2))


def _pad_cols(w, width):
    return jnp.pad(w, ((0, 0), (0, width - w.shape[1])))


def kernel(x, attn_norm_g, w_qkv_a, lambda_q1, lambda_k1, lambda_q2, lambda_k2, subln_g, w_o_a, kv_in_norm_g, w_dkv, kv_norm_g, w_ukv, w_dq, q_norm_g, w_uq, w_o_b, mlp_norm_g, w_up, w_down, final_norm_g):
    b, s, _ = x.shape
    t = b * s
    cos, sin = _rope_tables(s)
    row = lambda g: g.reshape(1, -1)
    h = x.reshape(t, D_MODEL)

    lambda_init = 0.8 - 0.6 * math.exp(-0.3 * 0)
    qz, k, v = _proj_a(h, row(attn_norm_g[0]), w_qkv_a[0].astype(BF16), cos, sin, s)
    lam_rows = [row(p[0]) for p in (lambda_q1, lambda_k1, lambda_q2, lambda_k2)]
    o = _attn_call(functools.partial(_diff_attn_kernel, lambda_init=lambda_init), 2,
                   qz.reshape(b, s, HEADS * SLOT), k.reshape(b, s, D_MODEL), LANES,
                   _values_transposed(v, b, s), [subln_g[0].reshape(V_DIM, 1)] + lam_rows, "diff_attn")
    h = _post(h, o.reshape(t, D_MODEL), w_o_a[0].astype(BF16), row(mlp_norm_g[0]),
              w_up[0].astype(BF16), w_down[0].astype(BF16), row(final_norm_g), False, "post_a")

    w_ukv_h = w_ukv.reshape(KV_LORA, HEADS, MLA_NOPE + V_DIM)
    w_uk = w_ukv_h[:, :, :MLA_NOPE].reshape(KV_LORA, HEADS * MLA_NOPE)
    w_uv = w_ukv_h[:, :, MLA_NOPE:].reshape(KV_LORA, HEADS * V_DIM)
    w_uq_slots = jnp.pad(w_uq[0].reshape(Q_LORA, HEADS, MLA_NOPE + MLA_ROPE),
                         ((0, 0), (0, 0), (0, SLOT - MLA_NOPE - MLA_ROPE))).reshape(Q_LORA, HEADS * SLOT)
    q, kk, v = _proj_b(h, row(kv_in_norm_g), _pad_cols(w_dkv, KV_LORA + LANES).astype(BF16), row(kv_norm_g),
                       w_uk.astype(BF16), w_uv.astype(BF16), row(attn_norm_g[1]), w_dq[0].astype(BF16),
                       row(q_norm_g[0]), w_uq_slots.astype(BF16), cos, sin, s)
    o = _attn_call(_mla_attn_kernel, 1, q.reshape(b, s, HEADS * SLOT), kk.reshape(b, s, HEADS * SLOT), SLOT,
                   _values_transposed(v, b, s), [], "mla_attn")
    h = _post(h, o.reshape(t, D_MODEL), w_o_b[0].astype(BF16), row(mlp_norm_g[1]),
              w_up[1].astype(BF16), w_down[1].astype(BF16), row(final_norm_g), True, "post_b")
    return h.reshape(b, s, D_MODEL)
```

```python
import functools
import math

import jax
import jax.numpy as jnp
from jax import lax
from jax.experimental import pallas as pl
from jax.experimental.pallas import tpu as pltpu

D_MODEL = 1024
DEPTH = 2
N_A = DEPTH // 2
HEADS = 8
HEAD_DIM = 64
V_DIM = 128
MLA_NOPE = 128
MLA_ROPE = 64
Q_LORA = 384
KV_LORA = 256
D_FF = 4 * D_MODEL
ROPE_THETA = 10000.0
EPS = 1e-6
SUBLN_EPS = 1e-5

LANES = 128
SLOT = 2 * LANES
TOKEN_TILE = 512
Q_TILE = 256
K_STEP = 512
Q_SUB = 4
STEPS_PER_SUPER = Q_SUB * Q_TILE // K_STEP
ONES_ROWS = 16
FF_CHUNK = 1024
NEG_BIG = -1e30
LOG2_E = math.log2(math.e)
VMEM_LIMIT = 56 * 1024 * 1024

BF16 = jnp.bfloat16
F32 = jnp.float32
NT_DIMS = (((1,), (1,)), ((), ()))


def _rms(x, g, eps):
    return x * lax.rsqrt(jnp.mean(x * x, axis=-1, keepdims=True) + eps) * g


def _rope_slab(x, cos, sin_signed):
    lane = lax.broadcasted_iota(jnp.int32, x.shape, 1)
    first = (lane % HEAD_DIM) < (HEAD_DIM // 2)
    swapped = jnp.where(first, pltpu.roll(x, LANES - HEAD_DIM // 2, 1), pltpu.roll(x, HEAD_DIM // 2, 1))
    return x * cos + swapped * sin_signed


def _dot(a, b):
    return jnp.dot(a, b, preferred_element_type=F32)


def _proj_a_kernel(x_ref, g_ref, w_ref, cos_ref, sin_ref, qz_ref, k_ref, v_ref):
    hn = _rms(x_ref[...], g_ref[...], EPS).astype(BF16)
    qkv = _dot(hn, w_ref[...])
    cos, sin = cos_ref[...], sin_ref[...]
    lane = lax.broadcasted_iota(jnp.int32, (x_ref.shape[0], LANES), 1)
    comp1 = lane < HEAD_DIM
    scale = HEAD_DIM ** -0.5 * LOG2_E
    for h in range(HEADS):
        q = _rope_slab(qkv[:, h * LANES:(h + 1) * LANES], cos, sin) * scale
        qz_ref[:, h * SLOT:h * SLOT + LANES] = jnp.where(comp1, q, 0.0).astype(BF16)
        qz_ref[:, h * SLOT + LANES:(h + 1) * SLOT] = jnp.where(comp1, 0.0, q).astype(BF16)
        k = _rope_slab(qkv[:, D_MODEL + h * LANES:D_MODEL + (h + 1) * LANES], cos, sin)
        k_ref[:, h * LANES:(h + 1) * LANES] = k.astype(BF16)
    v_ref[...] = qkv[:, 2 * D_MODEL:].astype(BF16)


def _proj_a(x2, g, w_qkv, cos, sin, seq):
    t = x2.shape[0]
    tm = TOKEN_TILE
    pos_blocks = seq // tm
    const = lambda i: (0, 0)
    return pl.pallas_call(
        _proj_a_kernel,
        grid=(t // tm,),
        in_specs=[
            pl.BlockSpec((tm, D_MODEL), lambda i: (i, 0)),
            pl.BlockSpec((1, D_MODEL), const),
            pl.BlockSpec((D_MODEL, 3 * D_MODEL), const),
            pl.BlockSpec((tm, LANES), lambda i: (i % pos_blocks, 0)),
            pl.BlockSpec((tm, LANES), lambda i: (i % pos_blocks, 0)),
        ],
        out_specs=[
            pl.BlockSpec((tm, HEADS * SLOT), lambda i: (i, 0)),
            pl.BlockSpec((tm, D_MODEL), lambda i: (i, 0)),
            pl.BlockSpec((tm, D_MODEL), lambda i: (i, 0)),
        ],
        out_shape=[
            jax.ShapeDtypeStruct((t, HEADS * SLOT), BF16),
            jax.ShapeDtypeStruct((t, D_MODEL), BF16),
            jax.ShapeDtypeStruct((t, D_MODEL), BF16),
        ],
        compiler_params=pltpu.CompilerParams(dimension_semantics=("arbitrary",), vmem_limit_bytes=VMEM_LIMIT),
        name="proj_a",
    )(x2, g, w_qkv, cos, sin)


def _causal_keep(offset):
    key = lax.broadcasted_iota(jnp.int32, (K_STEP, Q_TILE), 0)
    qry = lax.broadcasted_iota(jnp.int32, (K_STEP, Q_TILE), 1)
    return key <= qry + offset


def _diag_plan(e):
    plan = []
    for a in range(Q_SUB):
        if K_STEP * e <= Q_TILE * a + Q_TILE - 1:
            fully_visible = K_STEP * (e + 1) - 1 <= Q_TILE * a
            plan.append((a, None if fully_visible else Q_TILE * a - K_STEP * e))
    return plan


def _flash_super_tile(q_ref, q_cols, k_ref, vt_ref, s_ref, mx_ref, m_ref, acc_ref, sup):
    n = len(q_cols)
    m_ref[...] = jnp.full(m_ref.shape, NEG_BIG, F32)
    acc_ref[...] = jnp.zeros(acc_ref.shape, F32)
    first_tile = sup * Q_SUB
    first_step = sup * STEPS_PER_SUPER
    everyone = [(a, None) for a in range(Q_SUB)]
    ones = jnp.ones((ONES_ROWS, K_STEP), BF16)

    def scores(step, plan, slot):
        k_t = k_ref[0, pl.ds(pl.multiple_of(step * K_STEP, K_STEP), K_STEP), :]
        for a, _ in plan:
            rows = pl.ds(pl.multiple_of((first_tile + a) * Q_TILE, Q_TILE), Q_TILE)
            for c in range(n):
                s_t = lax.dot_general(k_t, q_ref[0, rows, q_cols[c]], NT_DIMS, preferred_element_type=F32)
                s_ref[slot, a * n + c] = s_t
                mx_ref[slot, a * n + c] = jnp.max(s_t, axis=0, keepdims=True)

    def update(step, plan, slot):
        vt_aug = jnp.concatenate([vt_ref[0, 0, step], ones], axis=0)
        for a, offset in plan:
            for c in range(n):
                ch = a * n + c
                if offset is None:
                    s_t, mx = s_ref[slot, ch], mx_ref[slot, ch]
                else:
                    s_t = jnp.where(_causal_keep(offset), s_ref[slot, ch], NEG_BIG)
                    mx = jnp.max(s_t, axis=0, keepdims=True)
                m_old = m_ref[ch]
                m_new = jnp.maximum(m_old, mx)
                p = jnp.exp2(s_t - m_new).astype(BF16)
                acc_ref[ch] = jnp.exp2(m_old - m_new) * acc_ref[ch] + _dot(vt_aug, p)
                m_ref[ch] = m_new

    scores(0, everyone, 0)

    def pair_body(jj, carry):
        step = 2 * jj
        scores(step + 1, everyone, 1)
        update(step, everyone, 0)
        scores(step + 2, everyone, 0)
        update(step + 1, everyone, 1)
        return carry

    lax.fori_loop(0, sup * (STEPS_PER_SUPER // 2), pair_body, 0)
    for e in range(STEPS_PER_SUPER):
        if e + 1 < STEPS_PER_SUPER:
            scores(first_step + e + 1, _diag_plan(e + 1), (e + 1) % 2)
        update(first_step + e, _diag_plan(e), e % 2)


def _softmax_out(acc_ref, ch):
    acc = acc_ref[ch]
    return acc[:V_DIM] * (1.0 / acc[V_DIM:V_DIM + 1])


def _diff_attn_kernel(qz_ref, k_ref, vt_ref, g_ref, lq1_ref, lk1_ref, lq2_ref, lk2_ref, o_ref,
                      s_ref, mx_ref, m_ref, acc_ref, *, lambda_init):
    lam = (jnp.exp(jnp.sum(lq1_ref[...] * lk1_ref[...], keepdims=True))
           - jnp.exp(jnp.sum(lq2_ref[...] * lk2_ref[...], keepdims=True)) + lambda_init)
    g_col = g_ref[...]

    def sup_body(sup, carry):
        _flash_super_tile(qz_ref, [slice(0, LANES), slice(LANES, SLOT)], k_ref, vt_ref,
                          s_ref, mx_ref, m_ref, acc_ref, sup)
        for a in range(Q_SUB):
            o_t = _softmax_out(acc_ref, 2 * a) - lam * _softmax_out(acc_ref, 2 * a + 1)
            y_t = o_t * lax.rsqrt(jnp.mean(o_t * o_t, axis=0, keepdims=True) + SUBLN_EPS) * g_col
            rows = pl.ds(pl.multiple_of((sup * Q_SUB + a) * Q_TILE, Q_TILE), Q_TILE)
            o_ref[0, rows, :] = (y_t * (1.0 - lambda_init)).T.astype(BF16)
        return carry

    lax.fori_loop(0, o_ref.shape[1] // (Q_TILE * Q_SUB), sup_body, 0)


def _mla_attn_kernel(q_ref, k_ref, vt_ref, o_ref, s_ref, mx_ref, m_ref, acc_ref):
    def sup_body(sup, carry):
        _flash_super_tile(q_ref, [slice(0, SLOT)], k_ref, vt_ref, s_ref, mx_ref, m_ref, acc_ref, sup)
        for a in range(Q_SUB):
            rows = pl.ds(pl.multiple_of((sup * Q_SUB + a) * Q_TILE, Q_TILE), Q_TILE)
            o_ref[0, rows, :] = _softmax_out(acc_ref, a).T.astype(BF16)
        return carry

    lax.fori_loop(0, o_ref.shape[1] // (Q_TILE * Q_SUB), sup_body, 0)


def _attn_call(kernel, n_softmax, q, k, k_width, vt, extra, name):
    b, s, _ = q.shape
    chains = n_softmax * Q_SUB
    small = lambda bi, hi: (0, 0)
    in_specs = [
        pl.BlockSpec((1, s, SLOT), lambda bi, hi: (bi, 0, hi)),
        pl.BlockSpec((1, s, k_width), lambda bi, hi: (bi, 0, hi)),
        pl.BlockSpec((1, 1, s // K_STEP, V_DIM, K_STEP), lambda bi, hi: (bi, hi, 0, 0, 0)),
    ] + [pl.BlockSpec(e.shape, small) for e in extra]
    return pl.pallas_call(
        kernel,
        grid=(b, HEADS),
        in_specs=in_specs,
        out_specs=pl.BlockSpec((1, s, V_DIM), lambda bi, hi: (bi, 0, hi)),
        out_shape=jax.ShapeDtypeStruct((b, s, HEADS * V_DIM), BF16),
        scratch_shapes=[
            pltpu.VMEM((2, chains, K_STEP, Q_TILE), F32),
            pltpu.VMEM((2, chains, 1, Q_TILE), F32),
            pltpu.VMEM((chains, 1, Q_TILE), F32),
            pltpu.VMEM((chains, V_DIM + ONES_ROWS, Q_TILE), F32),
        ],
        compiler_params=pltpu.CompilerParams(dimension_semantics=("arbitrary", "arbitrary"),
                                             vmem_limit_bytes=VMEM_LIMIT),
        name=name,
    )(q, k, vt, *extra)


def _values_transposed(v2, b, s):
    return v2.reshape(b, s // K_STEP, K_STEP, HEADS, V_DIM).transpose(0, 3, 1, 4, 2)


def _post_kernel(h_ref, o_ref, wo_ref, g_ref, wup_ref, wdn_ref, gf_ref, out_ref, *, final_norm):
    h1 = h_ref[...] + _dot(o_ref[...], wo_ref[...])
    hn = _rms(h1, g_ref[...], EPS).astype(BF16)
    acc = h1
    for f in range(D_FF // FF_CHUNK):
        cols = slice(f * FF_CHUNK, (f + 1) * FF_CHUNK)
        u = jnp.maximum(_dot(hn, wup_ref[:, cols]), 0.0)
        acc = acc + _dot((u * u).astype(BF16), wdn_ref[cols, :])
    if final_norm:
        acc = _rms(acc, gf_ref[...], EPS)
    out_ref[...] = acc


def _post(h2, o2, w_o, g_mlp, w_up, w_down, g_final, final_norm, name):
    t = h2.shape[0]
    tm = TOKEN_TILE
    const = lambda i: (0, 0)
    row = lambda i: (i, 0)
    resident = functools.partial(pl.BlockSpec, index_map=const, pipeline_mode=pl.Buffered(1))
    return pl.pallas_call(
        functools.partial(_post_kernel, final_norm=final_norm),
        grid=(t // tm,),
        in_specs=[
            pl.BlockSpec((tm, D_MODEL), row),
            pl.BlockSpec((tm, D_MODEL), row),
            resident((D_MODEL, D_MODEL)),
            pl.BlockSpec((1, D_MODEL), const),
            resident((D_MODEL, D_FF)),
            resident((D_FF, D_MODEL)),
            pl.BlockSpec((1, D_MODEL), const),
        ],
        out_specs=pl.BlockSpec((tm, D_MODEL), row),
        out_shape=jax.ShapeDtypeStruct((t, D_MODEL), F32),
        compiler_params=pltpu.CompilerParams(dimension_semantics=("arbitrary",), vmem_limit_bytes=VMEM_LIMIT),
        name=name,
    )(h2, o2, w_o, g_mlp, w_up, w_down, g_final)


def _proj_b_kernel(h_ref, gkv_ref, wdkv_ref, gc_ref, wuk_ref, wuv_ref, gq_ref, wdq_ref, gcq_ref, wuq_ref,
                   cos_ref, sin_ref, q_ref, k_ref, v_ref):
    h = h_ref[...]
    cos, sin = cos_ref[...], sin_ref[...]
    ckv = _dot(_rms(h, gkv_ref[...], EPS).astype(BF16), wdkv_ref[...])
    c = _rms(ckv[:, :KV_LORA], gc_ref[...], EPS).astype(BF16)
    k_rope = _rope_slab(ckv[:, KV_LORA:], cos, sin).astype(BF16)
    k_nope = _dot(c, wuk_ref[...]).astype(BF16)
    for hd in range(HEADS):
        k_ref[:, hd * SLOT:hd * SLOT + LANES] = k_nope[:, hd * LANES:(hd + 1) * LANES]
        k_ref[:, hd * SLOT + LANES:(hd + 1) * SLOT] = k_rope
    v_ref[...] = _dot(c, wuv_ref[...]).astype(BF16)
    cq = _rms(_dot(_rms(h, gq_ref[...], EPS).astype(BF16), wdq_ref[...]), gcq_ref[...], EPS).astype(BF16)
    q = _dot(cq, wuq_ref[...])
    scale = (MLA_NOPE + MLA_ROPE) ** -0.5 * LOG2_E
    for hd in range(HEADS):
        q_ref[:, hd * SLOT:hd * SLOT + LANES] = (q[:, hd * SLOT:hd * SLOT + LANES] * scale).astype(BF16)
        q_rope = _rope_slab(q[:, hd * SLOT + LANES:(hd + 1) * SLOT], cos, sin)
        q_ref[:, hd * SLOT + LANES:(hd + 1) * SLOT] = (q_rope * scale).astype(BF16)


def _proj_b(h2, g_kv_in, w_dkv, g_c, w_uk, w_uv, g_q, w_dq, g_cq, w_uq, cos, sin, seq):
    t = h2.shape[0]
    tm = TOKEN_TILE
    pos_blocks = seq // tm
    const = lambda i: (0, 0)
    row = lambda i: (i, 0)
    full = lambda a: pl.BlockSpec(a.shape, const)
    return pl.pallas_call(
        _proj_b_kernel,
        grid=(t // tm,),
        in_specs=[
            pl.BlockSpec((tm, D_MODEL), row),
            full(g_kv_in), full(w_dkv), full(g_c), full(w_uk), full(w_uv),
            full(g_q), full(w_dq), full(g_cq), full(w_uq),
            pl.BlockSpec((tm, LANES), lambda i: (i % pos_blocks, 0)),
            pl.BlockSpec((tm, LANES), lambda i: (i % pos_blocks, 0)),
        ],
        out_specs=[
            pl.BlockSpec((tm, HEADS * SLOT), row),
            pl.BlockSpec((tm, HEADS * SLOT), row),
            pl.BlockSpec((tm, HEADS * V_DIM), row),
        ],
        out_shape=[
            jax.ShapeDtypeStruct((t, HEADS * SLOT), BF16),
            jax.ShapeDtypeStruct((t, HEADS * SLOT), BF16),
            jax.ShapeDtypeStruct((t, HEADS * V_DIM), BF16),
        ],
        compiler_params=pltpu.CompilerParams(dimension_semantics=("arbitrary",), vmem_limit_bytes=VMEM_LIMIT),
        name="proj_b",
    )(h2, g_kv_in, w_dkv, g_c, w_uk, w_uv, g_q, w_dq, g_cq, w_uq, cos, sin)


def _rope_tables(seq):
    pos = jnp.arange(seq, dtype=F32)
    inv_freq = ROPE_THETA ** (-jnp.arange(0, HEAD_DIM, 2, dtype=F32) / HEAD_DIM)
    ang = pos[:, None] * inv_freq[None, :]
    cos, sin = jnp.cos(ang), jnp.sin(ang)
    return jnp.tile(cos, (1, 4)), jnp.tile(jnp.concatenate([-sin, sin], axis=1), (1, 2))


def _pad_cols(w, width):
    return jnp.pad(w, ((0, 0), (0, width - w.shape[1])))


def kernel(x, attn_norm_g, w_qkv_a, lambda_q1, lambda_k1, lambda_q2, lambda_k2, subln_g, w_o_a, kv_in_norm_g, w_dkv, kv_norm_g, w_ukv, w_dq, q_norm_g, w_uq, w_o_b, mlp_norm_g, w_up, w_down, final_norm_g):
    b, s, _ = x.shape
    t = b * s
    cos, sin = _rope_tables(s)
    row = lambda g: g.reshape(1, -1)
    h = x.reshape(t, D_MODEL)

    lambda_init = 0.8 - 0.6 * math.exp(-0.3 * 0)
    qz, k, v = _proj_a(h, row(attn_norm_g[0]), w_qkv_a[0].astype(BF16), cos, sin, s)
    lam_rows = [row(p[0]) for p in (lambda_q1, lambda_k1, lambda_q2, lambda_k2)]
    o = _attn_call(functools.partial(_diff_attn_kernel, lambda_init=lambda_init), 2,
                   qz.reshape(b, s, HEADS * SLOT), k.reshape(b, s, D_MODEL), LANES,
                   _values_transposed(v, b, s), [subln_g[0].reshape(V_DIM, 1)] + lam_rows, "diff_attn")
    h = _post(h, o.reshape(t, D_MODEL), w_o_a[0].astype(BF16), row(mlp_norm_g[0]),
              w_up[0].astype(BF16), w_down[0].astype(BF16), row(final_norm_g), False, "post_a")

    w_ukv_h = w_ukv.reshape(KV_LORA, HEADS, MLA_NOPE + V_DIM)
    w_uk = w_ukv_h[:, :, :MLA_NOPE].reshape(KV_LORA, HEADS * MLA_NOPE)
    w_uv = w_ukv_h[:, :, MLA_NOPE:].reshape(KV_LORA, HEADS * V_DIM)
    w_uq_slots = jnp.pad(w_uq[0].reshape(Q_LORA, HEADS, MLA_NOPE + MLA_ROPE),
                         ((0, 0), (0, 0), (0, SLOT - MLA_NOPE - MLA_ROPE))).reshape(Q_LORA, HEADS * SLOT)
    q, kk, v = _proj_b(h, row(kv_in_norm_g), _pad_cols(w_dkv, KV_LORA + LANES).astype(BF16), row(kv_norm_g),
                       w_uk.astype(BF16), w_uv.astype(BF16), row(attn_norm_g[1]), w_dq[0].astype(BF16),
                       row(q_norm_g[0]), w_uq_slots.astype(BF16), cos, sin, s)
    o = _attn_call(_mla_attn_kernel, 1, q.reshape(b, s, HEADS * SLOT), kk.reshape(b, s, HEADS * SLOT), SLOT,
                   _values_transposed(v, b, s), [], "mla_attn")
    h = _post(h, o.reshape(t, D_MODEL), w_o_b[0].astype(BF16), row(mlp_norm_g[1]),
              w_up[1].astype(BF16), w_down[1].astype(BF16), row(final_norm_g), True, "post_b")
    return h.reshape(b, s, D_MODEL)
```

```python
import functools
import math

import jax
import jax.numpy as jnp
from jax import lax
from jax.experimental import pallas as pl
from jax.experimental.pallas import tpu as pltpu

D_MODEL = 1024
DEPTH = 2
N_A = DEPTH // 2
HEADS = 8
HEAD_DIM = 64
V_DIM = 128
MLA_NOPE = 128
MLA_ROPE = 64
Q_LORA = 384
KV_LORA = 256
D_FF = 4 * D_MODEL
ROPE_THETA = 10000.0
EPS = 1e-6
SUBLN_EPS = 1e-5

LANES = 128
SLOT = 2 * LANES
TOKEN_TILE = 512
Q_TILE = 256
K_STEP = 512
Q_SUB = 4
STEPS_PER_SUPER = Q_SUB * Q_TILE // K_STEP
ONES_ROWS = 16
FF_CHUNK = 1024
NEG_BIG = -1e30
LOG2_E = math.log2(math.e)
VMEM_LIMIT = 56 * 1024 * 1024

BF16 = jnp.bfloat16
F32 = jnp.float32
NT_DIMS = (((1,), (1,)), ((), ()))


def _rms(x, g, eps):
    return x * lax.rsqrt(jnp.mean(x * x, axis=-1, keepdims=True) + eps) * g


def _rope_slab(x, cos, sin_signed):
    lane = lax.broadcasted_iota(jnp.int32, x.shape, 1)
    first = (lane % HEAD_DIM) < (HEAD_DIM // 2)
    swapped = jnp.where(first, pltpu.roll(x, LANES - HEAD_DIM // 2, 1), pltpu.roll(x, HEAD_DIM // 2, 1))
    return x * cos + swapped * sin_signed


def _dot(a, b):
    return jnp.dot(a, b, preferred_element_type=F32)


def _values_t_spec(steps_per_seq):
    assert TOKEN_TILE == K_STEP
    return pl.BlockSpec((1, HEADS, 1, V_DIM, K_STEP), lambda i: (i // steps_per_seq, 0, i % steps_per_seq, 0, 0))


def _values_t_shape(tokens, seq):
    return jax.ShapeDtypeStruct((tokens // seq, HEADS, seq // K_STEP, V_DIM, K_STEP), BF16)


def _proj_a_kernel(x_ref, g_ref, w_ref, cos_ref, sin_ref, qz_ref, k_ref, vt_ref):
    hn = _rms(x_ref[...], g_ref[...], EPS).astype(BF16)
    qkv = _dot(hn, w_ref[...])
    cos, sin = cos_ref[...], sin_ref[...]
    lane = lax.broadcasted_iota(jnp.int32, (x_ref.shape[0], LANES), 1)
    comp1 = lane < HEAD_DIM
    scale = HEAD_DIM ** -0.5 * LOG2_E
    for h in range(HEADS):
        q = _rope_slab(qkv[:, h * LANES:(h + 1) * LANES], cos, sin) * scale
        qz_ref[:, h * SLOT:h * SLOT + LANES] = jnp.where(comp1, q, 0.0).astype(BF16)
        qz_ref[:, h * SLOT + LANES:(h + 1) * SLOT] = jnp.where(comp1, 0.0, q).astype(BF16)
        k = _rope_slab(qkv[:, D_MODEL + h * LANES:D_MODEL + (h + 1) * LANES], cos, sin)
        k_ref[:, h * LANES:(h + 1) * LANES] = k.astype(BF16)
        v = qkv[:, 2 * D_MODEL + h * V_DIM:2 * D_MODEL + (h + 1) * V_DIM]
        vt_ref[0, h, 0] = v.T.astype(BF16)


def _proj_a(x2, g, w_qkv, cos, sin, seq):
    t = x2.shape[0]
    tm = TOKEN_TILE
    pos_blocks = seq // tm
    const = lambda i: (0, 0)
    return pl.pallas_call(
        _proj_a_kernel,
        grid=(t // tm,),
        in_specs=[
            pl.BlockSpec((tm, D_MODEL), lambda i: (i, 0)),
            pl.BlockSpec((1, D_MODEL), const),
            pl.BlockSpec((D_MODEL, 3 * D_MODEL), const),
            pl.BlockSpec((tm, LANES), lambda i: (i % pos_blocks, 0)),
            pl.BlockSpec((tm, LANES), lambda i: (i % pos_blocks, 0)),
        ],
        out_specs=[
            pl.BlockSpec((tm, HEADS * SLOT), lambda i: (i, 0)),
            pl.BlockSpec((tm, D_MODEL), lambda i: (i, 0)),
            _values_t_spec(pos_blocks),
        ],
        out_shape=[
            jax.ShapeDtypeStruct((t, HEADS * SLOT), BF16),
            jax.ShapeDtypeStruct((t, D_MODEL), BF16),
            _values_t_shape(t, seq),
        ],
        compiler_params=pltpu.CompilerParams(dimension_semantics=("arbitrary",), vmem_limit_bytes=VMEM_LIMIT),
        name="proj_a",
    )(x2, g, w_qkv, cos, sin)


def _causal_keep(offset):
    key = lax.broadcasted_iota(jnp.int32, (K_STEP, Q_TILE), 0)
    qry = lax.broadcasted_iota(jnp.int32, (K_STEP, Q_TILE), 1)
    return key <= qry + offset


def _diag_plan(e):
    plan = []
    for a in range(Q_SUB):
        if K_STEP * e <= Q_TILE * a + Q_TILE - 1:
            fully_visible = K_STEP * (e + 1) - 1 <= Q_TILE * a
            plan.append((a, None if fully_visible else Q_TILE * a - K_STEP * e))
    return plan


def _score_chain(q_ref, q_cols, k_ref, s_ref, mx_ref, sup, step, a, c, slot):
    k_t = k_ref[0, pl.ds(pl.multiple_of(step * K_STEP, K_STEP), K_STEP), :]
    rows = pl.ds(pl.multiple_of((sup * Q_SUB + a) * Q_TILE, Q_TILE), Q_TILE)
    s_t = lax.dot_general(k_t, q_ref[0, rows, q_cols[c]], NT_DIMS, preferred_element_type=F32)
    ch = a * len(q_cols) + c
    s_ref[slot, ch] = s_t
    mx_ref[slot, ch] = jnp.max(s_t, axis=0, keepdims=True)


def _flash_scores(q_ref, q_cols, k_ref, s_ref, mx_ref, sup, step, plan, slot):
    for a, _ in plan:
        for c in range(len(q_cols)):
            _score_chain(q_ref, q_cols, k_ref, s_ref, mx_ref, sup, step, a, c, slot)


EVERYONE = [(a, None) for a in range(Q_SUB)]


def _flash_super_tile(q_ref, q_cols, k_ref, vt_ref, s_ref, mx_ref, m_ref, acc_ref, sup, prefetch_next, finalize):
    n = len(q_cols)
    m_ref[...] = jnp.full(m_ref.shape, NEG_BIG, F32)
    acc_ref[...] = jnp.zeros(acc_ref.shape, F32)
    first_step = sup * STEPS_PER_SUPER
    ones = jnp.ones((ONES_ROWS, K_STEP), BF16)

    def scores(step, plan, slot):
        _flash_scores(q_ref, q_cols, k_ref, s_ref, mx_ref, sup, step, plan, slot)

    def update(step, plan, slot, after_chain=None):
        vt_aug = jnp.concatenate([vt_ref[0, 0, step], ones], axis=0)
        for a, offset in plan:
            for c in range(n):
                ch = a * n + c
                if offset is None:
                    s_t, mx = s_ref[slot, ch], mx_ref[slot, ch]
                else:
                    s_t = jnp.where(_causal_keep(offset), s_ref[slot, ch], NEG_BIG)
                    mx = jnp.max(s_t, axis=0, keepdims=True)
                m_old = m_ref[ch]
                m_new = jnp.maximum(m_old, mx)
                p = jnp.exp2(s_t - m_new).astype(BF16)
                acc_ref[ch] = jnp.exp2(m_old - m_new) * acc_ref[ch] + _dot(vt_aug, p)
                m_ref[ch] = m_new
                if after_chain is not None:
                    after_chain(a, c)

    def pair_body(jj, carry):
        step = 2 * jj
        scores(step + 1, EVERYONE, 1)
        update(step, EVERYONE, 0)
        scores(step + 2, EVERYONE, 0)
        update(step + 1, EVERYONE, 1)
        return carry

    lax.fori_loop(0, sup * (STEPS_PER_SUPER // 2), pair_body, 0)
    for e in range(STEPS_PER_SUPER):
        plan = _diag_plan(e)
        if e + 1 < STEPS_PER_SUPER:
            scores(first_step + e + 1, _diag_plan(e + 1), (e + 1) % 2)
        refill = None
        if prefetch_next and e == STEPS_PER_SUPER - 2:
            assert len(plan) == Q_SUB and e % 2 == 0
            refill = functools.partial(_score_chain, q_ref, q_cols, k_ref, s_ref, mx_ref, sup + 1, 0, slot=0)
        update(first_step + e, plan, e % 2, refill)
        for a in range(Q_SUB):
            if (Q_TILE * a + Q_TILE - 1) // K_STEP == e:
                finalize(a)


def _run_super_tiles(super_tile, q_ref, q_cols, k_ref, s_ref, mx_ref, n_super):
    _flash_scores(q_ref, q_cols, k_ref, s_ref, mx_ref, 0, 0, EVERYONE, 0)

    def body(sup, carry):
        super_tile(sup, True)
        return carry

    lax.fori_loop(0, n_super - 1, body, 0)
    super_tile(n_super - 1, False)


def _softmax_out(acc_ref, ch):
    acc = acc_ref[ch]
    return acc[:V_DIM] * (1.0 / acc[V_DIM:V_DIM + 1])


def _diff_attn_kernel(qz_ref, k_ref, vt_ref, g_ref, lq1_ref, lk1_ref, lq2_ref, lk2_ref, o_ref,
                      s_ref, mx_ref, m_ref, acc_ref, *, lambda_init):
    lam = (jnp.exp(jnp.sum(lq1_ref[...] * lk1_ref[...], keepdims=True))
           - jnp.exp(jnp.sum(lq2_ref[...] * lk2_ref[...], keepdims=True)) + lambda_init)
    g_col = g_ref[...]
    q_cols = [slice(0, LANES), slice(LANES, SLOT)]

    def super_tile(sup, prefetch_next):
        def finalize(a):
            o_t = _softmax_out(acc_ref, 2 * a) - lam * _softmax_out(acc_ref, 2 * a + 1)
            y_t = o_t * lax.rsqrt(jnp.mean(o_t * o_t, axis=0, keepdims=True) + SUBLN_EPS) * g_col
            rows = pl.ds(pl.multiple_of((sup * Q_SUB + a) * Q_TILE, Q_TILE), Q_TILE)
            o_ref[0, rows, :] = (y_t * (1.0 - lambda_init)).T.astype(BF16)

        _flash_super_tile(qz_ref, q_cols, k_ref, vt_ref, s_ref, mx_ref, m_ref, acc_ref, sup, prefetch_next, finalize)

    _run_super_tiles(super_tile, qz_ref, q_cols, k_ref, s_ref, mx_ref, o_ref.shape[1] // (Q_TILE * Q_SUB))


def _mla_attn_kernel(q_ref, k_ref, vt_ref, o_ref, s_ref, mx_ref, m_ref, acc_ref):
    q_cols = [slice(0, SLOT)]

    def super_tile(sup, prefetch_next):
        def finalize(a):
            rows = pl.ds(pl.multiple_of((sup * Q_SUB + a) * Q_TILE, Q_TILE), Q_TILE)
            o_ref[0, rows, :] = _softmax_out(acc_ref, a).T.astype(BF16)

        _flash_super_tile(q_ref, q_cols, k_ref, vt_ref, s_ref, mx_ref, m_ref, acc_ref, sup, prefetch_next, finalize)

    _run_super_tiles(super_tile, q_ref, q_cols, k_ref, s_ref, mx_ref, o_ref.shape[1] // (Q_TILE * Q_SUB))


def _attn_call(kernel, n_softmax, q, k, k_width, vt, extra, name):
    b, s, _ = q.shape
    chains = n_softmax * Q_SUB
    small = lambda bi, hi: (0, 0)
    in_specs = [
        pl.BlockSpec((1, s, SLOT), lambda bi, hi: (bi, 0, hi)),
        pl.BlockSpec((1, s, k_width), lambda bi, hi: (bi, 0, hi)),
        pl.BlockSpec((1, 1, s // K_STEP, V_DIM, K_STEP), lambda bi, hi: (bi, hi, 0, 0, 0)),
    ] + [pl.BlockSpec(e.shape, small) for e in extra]
    return pl.pallas_call(
        kernel,
        grid=(b, HEADS),
        in_specs=in_specs,
        out_specs=pl.BlockSpec((1, s, V_DIM), lambda bi, hi: (bi, 0, hi)),
        out_shape=jax.ShapeDtypeStruct((b, s, HEADS * V_DIM), BF16),
        scratch_shapes=[
            pltpu.VMEM((2, chains, K_STEP, Q_TILE), F32),
            pltpu.VMEM((2, chains, 1, Q_TILE), F32),
            pltpu.VMEM((chains, 1, Q_TILE), F32),
            pltpu.VMEM((chains, V_DIM + ONES_ROWS, Q_TILE), F32),
        ],
        compiler_params=pltpu.CompilerParams(dimension_semantics=("arbitrary", "arbitrary"),
                                             vmem_limit_bytes=VMEM_LIMIT),
        name=name,
    )(q, k, vt, *extra)


def _post_kernel(h_ref, o_ref, wo_ref, g_ref, wup_ref, wdn_ref, gf_ref, out_ref, *, final_norm):
    h1 = h_ref[...] + _dot(o_ref[...], wo_ref[...])
    hn = _rms(h1, g_ref[...], EPS).astype(BF16)
    acc = h1
    for f in range(D_FF // FF_CHUNK):
        cols = slice(f * FF_CHUNK, (f + 1) * FF_CHUNK)
        u = jnp.maximum(_dot(hn, wup_ref[:, cols]), 0.0)
        acc = acc + _dot((u * u).astype(BF16), wdn_ref[cols, :])
    if final_norm:
        acc = _rms(acc, gf_ref[...], EPS)
    out_ref[...] = acc


def _post(h2, o2, w_o, g_mlp, w_up, w_down, g_final, final_norm, name):
    t = h2.shape[0]
    tm = TOKEN_TILE
    const = lambda i: (0, 0)
    row = lambda i: (i, 0)
    resident = functools.partial(pl.BlockSpec, index_map=const, pipeline_mode=pl.Buffered(1))
    return pl.pallas_call(
        functools.partial(_post_kernel, final_norm=final_norm),
        grid=(t // tm,),
        in_specs=[
            pl.BlockSpec((tm, D_MODEL), row),
            pl.BlockSpec((tm, D_MODEL), row),
            resident((D_MODEL, D_MODEL)),
            pl.BlockSpec((1, D_MODEL), const),
            resident((D_MODEL, D_FF)),
            resident((D_FF, D_MODEL)),
            pl.BlockSpec((1, D_MODEL), const),
        ],
        out_specs=pl.BlockSpec((tm, D_MODEL), row),
        out_shape=jax.ShapeDtypeStruct((t, D_MODEL), F32),
        compiler_params=pltpu.CompilerParams(dimension_semantics=("arbitrary",), vmem_limit_bytes=VMEM_LIMIT),
        name=name,
    )(h2, o2, w_o, g_mlp, w_up, w_down, g_final)


def _proj_b_kernel(h_ref, gkv_ref, wdkv_ref, gc_ref, wuk_ref, wuv_ref, gq_ref, wdq_ref, gcq_ref, wuq_ref,
                   cos_ref, sin_ref, q_ref, k_ref, vt_ref):
    h = h_ref[...]
    cos, sin = cos_ref[...], sin_ref[...]
    ckv = _dot(_rms(h, gkv_ref[...], EPS).astype(BF16), wdkv_ref[...])
    c = _rms(ckv[:, :KV_LORA], gc_ref[...], EPS).astype(BF16)
    k_rope = _rope_slab(ckv[:, KV_LORA:], cos, sin).astype(BF16)
    k_nope = _dot(c, wuk_ref[...]).astype(BF16)
    for hd in range(HEADS):
        k_ref[:, hd * SLOT:hd * SLOT + LANES] = k_nope[:, hd * LANES:(hd + 1) * LANES]
        k_ref[:, hd * SLOT + LANES:(hd + 1) * SLOT] = k_rope
    v = _dot(c, wuv_ref[...])
    for hd in range(HEADS):
        vt_ref[0, hd, 0] = v[:, hd * V_DIM:(hd + 1) * V_DIM].T.astype(BF16)
    cq = _rms(_dot(_rms(h, gq_ref[...], EPS).astype(BF16), wdq_ref[...]), gcq_ref[...], EPS).astype(BF16)
    q = _dot(cq, wuq_ref[...])
    scale = (MLA_NOPE + MLA_ROPE) ** -0.5 * LOG2_E
    for hd in range(HEADS):
        q_ref[:, hd * SLOT:hd * SLOT + LANES] = (q[:, hd * SLOT:hd * SLOT + LANES] * scale).astype(BF16)
        q_rope = _rope_slab(q[:, hd * SLOT + LANES:(hd + 1) * SLOT], cos, sin)
        q_ref[:, hd * SLOT + LANES:(hd + 1) * SLOT] = (q_rope * scale).astype(BF16)


def _proj_b(h2, g_kv_in, w_dkv, g_c, w_uk, w_uv, g_q, w_dq, g_cq, w_uq, cos, sin, seq):
    t = h2.shape[0]
    tm = TOKEN_TILE
    pos_blocks = seq // tm
    const = lambda i: (0, 0)
    row = lambda i: (i, 0)
    full = lambda a: pl.BlockSpec(a.shape, const)
    return pl.pallas_call(
        _proj_b_kernel,
        grid=(t // tm,),
        in_specs=[
            pl.BlockSpec((tm, D_MODEL), row),
            full(g_kv_in), full(w_dkv), full(g_c), full(w_uk), full(w_uv),
            full(g_q), full(w_dq), full(g_cq), full(w_uq),
            pl.BlockSpec((tm, LANES), lambda i: (i % pos_blocks, 0)),
            pl.BlockSpec((tm, LANES), lambda i: (i % pos_blocks, 0)),
        ],
        out_specs=[
            pl.BlockSpec((tm, HEADS * SLOT), row),
            pl.BlockSpec((tm, HEADS * SLOT), row),
            _values_t_spec(pos_blocks),
        ],
        out_shape=[
            jax.ShapeDtypeStruct((t, HEADS * SLOT), BF16),
            jax.ShapeDtypeStruct((t, HEADS * SLOT), BF16),
            _values_t_shape(t, seq),
        ],
        compiler_params=pltpu.CompilerParams(dimension_semantics=("arbitrary",), vmem_limit_bytes=VMEM_LIMIT),
        name="proj_b",
    )(h2, g_kv_in, w_dkv, g_c, w_uk, w_uv, g_q, w_dq, g_cq, w_uq, cos, sin)


def _rope_tables(seq):
    pos = jnp.arange(seq, dtype=F32)
    inv_freq = ROPE_THETA ** (-jnp.arange(0, HEAD_DIM, 2, dtype=F32) / HEAD_DIM)
    ang = pos[:, None] * inv_freq[None, :]
    cos, sin = jnp.cos(ang), jnp.sin(ang)
    return jnp.tile(cos, (1, 4)), jnp.tile(jnp.concatenate([-sin, sin], axis=1), (1, 2))


def _pad_cols(w, width):
    return jnp.pad(w, ((0, 0), (0, width - w.shape[1])))


def kernel(x, attn_norm_g, w_qkv_a, lambda_q1, lambda_k1, lambda_q2, lambda_k2, subln_g, w_o_a, kv_in_norm_g, w_dkv, kv_norm_g, w_ukv, w_dq, q_norm_g, w_uq, w_o_b, mlp_norm_g, w_up, w_down, final_norm_g):
    b, s, _ = x.shape
    t = b * s
    cos, sin = _rope_tables(s)
    row = lambda g: g.reshape(1, -1)
    h = x.reshape(t, D_MODEL)

    lambda_init = 0.8 - 0.6 * math.exp(-0.3 * 0)
    qz, k, vt = _proj_a(h, row(attn_norm_g[0]), w_qkv_a[0].astype(BF16), cos, sin, s)
    lam_rows = [row(p[0]) for p in (lambda_q1, lambda_k1, lambda_q2, lambda_k2)]
    o = _attn_call(functools.partial(_diff_attn_kernel, lambda_init=lambda_init), 2,
                   qz.reshape(b, s, HEADS * SLOT), k.reshape(b, s, D_MODEL), LANES,
                   vt, [subln_g[0].reshape(V_DIM, 1)] + lam_rows, "diff_attn")
    h = _post(h, o.reshape(t, D_MODEL), w_o_a[0].astype(BF16), row(mlp_norm_g[0]),
              w_up[0].astype(BF16), w_down[0].astype(BF16), row(final_norm_g), False, "post_a")

    w_ukv_h = w_ukv.reshape(KV_LORA, HEADS, MLA_NOPE + V_DIM)
    w_uk = w_ukv_h[:, :, :MLA_NOPE].reshape(KV_LORA, HEADS * MLA_NOPE)
    w_uv = w_ukv_h[:, :, MLA_NOPE:].reshape(KV_LORA, HEADS * V_DIM)
    w_uq_slots = jnp.pad(w_uq[0].reshape(Q_LORA, HEADS, MLA_NOPE + MLA_ROPE),
                         ((0, 0), (0, 0), (0, SLOT - MLA_NOPE - MLA_ROPE))).reshape(Q_LORA, HEADS * SLOT)
    q, kk, vt = _proj_b(h, row(kv_in_norm_g), _pad_cols(w_dkv, KV_LORA + LANES).astype(BF16), row(kv_norm_g),
                       w_uk.astype(BF16), w_uv.astype(BF16), row(attn_norm_g[1]), w_dq[0].astype(BF16),
                       row(q_norm_g[0]), w_uq_slots.astype(BF16), cos, sin, s)
    o = _attn_call(_mla_attn_kernel, 1, q.reshape(b, s, HEADS * SLOT), kk.reshape(b, s, HEADS * SLOT), SLOT,
                   vt, [], "mla_attn")
    h = _post(h, o.reshape(t, D_MODEL), w_o_b[0].astype(BF16), row(mlp_norm_g[1]),
              w_up[1].astype(BF16), w_down[1].astype(BF16), row(final_norm_g), True, "post_b")
    return h.reshape(b, s, D_MODEL)
```

```python
import functools
import math

import jax
import jax.numpy as jnp
from jax import lax
from jax.experimental import pallas as pl
from jax.experimental.pallas import tpu as pltpu

D_MODEL = 1024
DEPTH = 2
N_A = DEPTH // 2
HEADS = 8
HEAD_DIM = 64
V_DIM = 128
MLA_NOPE = 128
MLA_ROPE = 64
Q_LORA = 384
KV_LORA = 256
D_FF = 4 * D_MODEL
ROPE_THETA = 10000.0
EPS = 1e-6
SUBLN_EPS = 1e-5

LANES = 128
SLOT = 2 * LANES
TOKEN_TILE = 512
Q_TILE = 256
K_STEP = 512
Q_SUB = 8
STEPS_PER_SUPER = Q_SUB * Q_TILE // K_STEP
ONES_ROWS = 16
FF_CHUNK = 1024
NEG_BIG = -1e30
LOG2_E = math.log2(math.e)
VMEM_LIMIT = 56 * 1024 * 1024

BF16 = jnp.bfloat16
F32 = jnp.float32
NT_DIMS = (((1,), (1,)), ((), ()))


def _rms(x, g, eps):
    return x * lax.rsqrt(jnp.mean(x * x, axis=-1, keepdims=True) + eps) * g


def _rope_slab(x, cos, sin_signed):
    lane = lax.broadcasted_iota(jnp.int32, x.shape, 1)
    first = (lane % HEAD_DIM) < (HEAD_DIM // 2)
    swapped = jnp.where(first, pltpu.roll(x, LANES - HEAD_DIM // 2, 1), pltpu.roll(x, HEAD_DIM // 2, 1))
    return x * cos + swapped * sin_signed


def _dot(a, b):
    return jnp.dot(a, b, preferred_element_type=F32)


def _values_t_spec(steps_per_seq):
    assert TOKEN_TILE == K_STEP
    return pl.BlockSpec((1, HEADS, 1, V_DIM, K_STEP), lambda i: (i // steps_per_seq, 0, i % steps_per_seq, 0, 0))


def _values_t_shape(tokens, seq):
    return jax.ShapeDtypeStruct((tokens // seq, HEADS, seq // K_STEP, V_DIM, K_STEP), BF16)


def _proj_a_kernel(x_ref, g_ref, w_ref, cos_ref, sin_ref, qz_ref, k_ref, vt_ref):
    hn = _rms(x_ref[...], g_ref[...], EPS).astype(BF16)
    qkv = _dot(hn, w_ref[...])
    cos, sin = cos_ref[...], sin_ref[...]
    lane = lax.broadcasted_iota(jnp.int32, (x_ref.shape[0], LANES), 1)
    comp1 = lane < HEAD_DIM
    scale = HEAD_DIM ** -0.5 * LOG2_E
    for h in range(HEADS):
        q = _rope_slab(qkv[:, h * LANES:(h + 1) * LANES], cos, sin) * scale
        qz_ref[:, h * SLOT:h * SLOT + LANES] = jnp.where(comp1, q, 0.0).astype(BF16)
        qz_ref[:, h * SLOT + LANES:(h + 1) * SLOT] = jnp.where(comp1, 0.0, q).astype(BF16)
        k = _rope_slab(qkv[:, D_MODEL + h * LANES:D_MODEL + (h + 1) * LANES], cos, sin)
        k_ref[:, h * LANES:(h + 1) * LANES] = k.astype(BF16)
        v = qkv[:, 2 * D_MODEL + h * V_DIM:2 * D_MODEL + (h + 1) * V_DIM]
        vt_ref[0, h, 0] = v.T.astype(BF16)


def _proj_a(x2, g, w_qkv, cos, sin, seq):
    t = x2.shape[0]
    tm = TOKEN_TILE
    pos_blocks = seq // tm
    const = lambda i: (0, 0)
    return pl.pallas_call(
        _proj_a_kernel,
        grid=(t // tm,),
        in_specs=[
            pl.BlockSpec((tm, D_MODEL), lambda i: (i, 0)),
            pl.BlockSpec((1, D_MODEL), const),
            pl.BlockSpec((D_MODEL, 3 * D_MODEL), const),
            pl.BlockSpec((tm, LANES), lambda i: (i % pos_blocks, 0)),
            pl.BlockSpec((tm, LANES), lambda i: (i % pos_blocks, 0)),
        ],
        out_specs=[
            pl.BlockSpec((tm, HEADS * SLOT), lambda i: (i, 0)),
            pl.BlockSpec((tm, D_MODEL), lambda i: (i, 0)),
            _values_t_spec(pos_blocks),
        ],
        out_shape=[
            jax.ShapeDtypeStruct((t, HEADS * SLOT), BF16),
            jax.ShapeDtypeStruct((t, D_MODEL), BF16),
            _values_t_shape(t, seq),
        ],
        compiler_params=pltpu.CompilerParams(dimension_semantics=("arbitrary",), vmem_limit_bytes=VMEM_LIMIT),
        name="proj_a",
    )(x2, g, w_qkv, cos, sin)


def _causal_keep(offset, n_keys):
    key = lax.broadcasted_iota(jnp.int32, (n_keys, Q_TILE), 0)
    qry = lax.broadcasted_iota(jnp.int32, (n_keys, Q_TILE), 1)
    return key <= qry + offset


def _diag_plan(e):
    plan = []
    for a in range(Q_SUB):
        offset = Q_TILE * a - K_STEP * e
        if offset + Q_TILE > 0:
            plan.append((a, None if offset >= K_STEP - 1 else offset, min(K_STEP, offset + Q_TILE)))
    return plan


def _score_chain(q_ref, q_cols, k_ref, s_ref, mx_ref, sup, step, a, c, slot, n_keys=K_STEP):
    k_t = k_ref[0, pl.ds(pl.multiple_of(step * K_STEP, K_STEP), n_keys), :]
    rows = pl.ds(pl.multiple_of((sup * Q_SUB + a) * Q_TILE, Q_TILE), Q_TILE)
    s_t = lax.dot_general(k_t, q_ref[0, rows, q_cols[c]], NT_DIMS, preferred_element_type=F32)
    ch = a * len(q_cols) + c
    s_ref[slot, ch, :n_keys] = s_t
    mx_ref[slot, ch] = jnp.max(s_t, axis=0, keepdims=True)


def _flash_scores(q_ref, q_cols, k_ref, s_ref, mx_ref, sup, step, plan, slot):
    for a, _, n_keys in plan:
        for c in range(len(q_cols)):
            _score_chain(q_ref, q_cols, k_ref, s_ref, mx_ref, sup, step, a, c, slot, n_keys)


EVERYONE = [(a, None, K_STEP) for a in range(Q_SUB)]


def _flash_super_tile(q_ref, q_cols, k_ref, vt_ref, s_ref, mx_ref, m_ref, acc_ref, sup, prefetch_next, finalize):
    n = len(q_cols)
    m_ref[...] = jnp.full(m_ref.shape, NEG_BIG, F32)
    acc_ref[...] = jnp.zeros(acc_ref.shape, F32)
    first_step = sup * STEPS_PER_SUPER
    ones = jnp.ones((ONES_ROWS, K_STEP), BF16)

    def scores(step, plan, slot):
        _flash_scores(q_ref, q_cols, k_ref, s_ref, mx_ref, sup, step, plan, slot)

    def update(step, plan, slot, after_chain=None, before_chain=None):
        vt_aug = jnp.concatenate([vt_ref[0, 0, step], ones], axis=0)
        for a, offset, n_keys in plan:
            for c in range(n):
                ch = a * n + c
                if before_chain is not None:
                    before_chain(a, c)
                if offset is None:
                    s_t, mx = s_ref[slot, ch], mx_ref[slot, ch]
                else:
                    s_t = jnp.where(_causal_keep(offset, n_keys), s_ref[slot, ch, :n_keys], NEG_BIG)
                    mx = jnp.max(s_t, axis=0, keepdims=True)
                m_old = m_ref[ch]
                m_new = jnp.maximum(m_old, mx)
                p = jnp.exp2(s_t - m_new).astype(BF16)
                acc_ref[ch] = jnp.exp2(m_old - m_new) * acc_ref[ch] + _dot(vt_aug[:, :n_keys], p)
                m_ref[ch] = m_new
                if after_chain is not None:
                    after_chain(a, c)

    def pair_body(jj, carry):
        step = 2 * jj
        update(step, EVERYONE, 0, before_chain=functools.partial(
            _score_chain, q_ref, q_cols, k_ref, s_ref, mx_ref, sup, step + 1, slot=1))
        update(step + 1, EVERYONE, 1, before_chain=functools.partial(
            _score_chain, q_ref, q_cols, k_ref, s_ref, mx_ref, sup, step + 2, slot=0))
        return carry

    lax.fori_loop(0, sup * (STEPS_PER_SUPER // 2), pair_body, 0)
    for e in range(STEPS_PER_SUPER):
        plan = _diag_plan(e)
        if e + 1 < STEPS_PER_SUPER:
            scores(first_step + e + 1, _diag_plan(e + 1), (e + 1) % 2)
        refill = None
        if prefetch_next and e == STEPS_PER_SUPER - 2:
            assert e % 2 == 0
            refill = functools.partial(_score_chain, q_ref, q_cols, k_ref, s_ref, mx_ref, sup + 1, 0, slot=0)
            for a in range(Q_SUB):
                if a not in [b for b, _, _ in plan]:
                    for c in range(n):
                        refill(a, c)
        update(first_step + e, plan, e % 2, refill)
        for a in range(Q_SUB):
            if (Q_TILE * a + Q_TILE - 1) // K_STEP == e:
                finalize(a)


def _run_super_tiles(super_tile, q_ref, q_cols, k_ref, s_ref, mx_ref, n_super):
    _flash_scores(q_ref, q_cols, k_ref, s_ref, mx_ref, 0, 0, _diag_plan(0), 0)

    def body(sup, carry):
        super_tile(sup, True)
        return carry

    lax.fori_loop(0, n_super - 1, body, 0)
    super_tile(n_super - 1, False)


def _softmax_out(acc_ref, ch):
    acc = acc_ref[ch]
    return acc[:V_DIM] * (1.0 / acc[V_DIM:V_DIM + 1])


def _diff_attn_kernel(qz_ref, k_ref, vt_ref, g_ref, lq1_ref, lk1_ref, lq2_ref, lk2_ref, o_ref,
                      s_ref, mx_ref, m_ref, acc_ref, *, lambda_init):
    lam = (jnp.exp(jnp.sum(lq1_ref[...] * lk1_ref[...], keepdims=True))
           - jnp.exp(jnp.sum(lq2_ref[...] * lk2_ref[...], keepdims=True)) + lambda_init)
    g_col = g_ref[...]
    q_cols = [slice(0, LANES), slice(LANES, SLOT)]

    def super_tile(sup, prefetch_next):
        def finalize(a):
            o_t = _softmax_out(acc_ref, 2 * a) - lam * _softmax_out(acc_ref, 2 * a + 1)
            y_t = o_t * lax.rsqrt(jnp.mean(o_t * o_t, axis=0, keepdims=True) + SUBLN_EPS) * g_col
            rows = pl.ds(pl.multiple_of((sup * Q_SUB + a) * Q_TILE, Q_TILE), Q_TILE)
            o_ref[0, rows, :] = (y_t * (1.0 - lambda_init)).T.astype(BF16)

        _flash_super_tile(qz_ref, q_cols, k_ref, vt_ref, s_ref, mx_ref, m_ref, acc_ref, sup, prefetch_next, finalize)

    _run_super_tiles(super_tile, qz_ref, q_cols, k_ref, s_ref, mx_ref, o_ref.shape[1] // (Q_TILE * Q_SUB))


def _mla_attn_kernel(q_ref, k_ref, vt_ref, o_ref, s_ref, mx_ref, m_ref, acc_ref):
    q_cols = [slice(0, SLOT)]

    def super_tile(sup, prefetch_next):
        def finalize(a):
            rows = pl.ds(pl.multiple_of((sup * Q_SUB + a) * Q_TILE, Q_TILE), Q_TILE)
            o_ref[0, rows, :] = _softmax_out(acc_ref, a).T.astype(BF16)

        _flash_super_tile(q_ref, q_cols, k_ref, vt_ref, s_ref, mx_ref, m_ref, acc_ref, sup, prefetch_next, finalize)

    _run_super_tiles(super_tile, q_ref, q_cols, k_ref, s_ref, mx_ref, o_ref.shape[1] // (Q_TILE * Q_SUB))


def _attn_call(kernel, n_softmax, q, k, k_width, vt, extra, name):
    b, s, _ = q.shape
    chains = n_softmax * Q_SUB
    small = lambda bi, hi: (0, 0)
    in_specs = [
        pl.BlockSpec((1, s, SLOT), lambda bi, hi: (bi, 0, hi)),
        pl.BlockSpec((1, s, k_width), lambda bi, hi: (bi, 0, hi)),
        pl.BlockSpec((1, 1, s // K_STEP, V_DIM, K_STEP), lambda bi, hi: (bi, hi, 0, 0, 0)),
    ] + [pl.BlockSpec(e.shape, small) for e in extra]
    return pl.pallas_call(
        kernel,
        grid=(b, HEADS),
        in_specs=in_specs,
        out_specs=pl.BlockSpec((1, s, V_DIM), lambda bi, hi: (bi, 0, hi)),
        out_shape=jax.ShapeDtypeStruct((b, s, HEADS * V_DIM), BF16),
        scratch_shapes=[
            pltpu.VMEM((2, chains, K_STEP, Q_TILE), F32),
            pltpu.VMEM((2, chains, 1, Q_TILE), F32),
            pltpu.VMEM((chains, 1, Q_TILE), F32),
            pltpu.VMEM((chains, V_DIM + ONES_ROWS, Q_TILE), F32),
        ],
        compiler_params=pltpu.CompilerParams(dimension_semantics=("arbitrary", "arbitrary"),
                                             vmem_limit_bytes=VMEM_LIMIT),
        name=name,
    )(q, k, vt, *extra)


def _post_kernel(h_ref, o_ref, wo_ref, g_ref, wup_ref, wdn_ref, gf_ref, out_ref, *, final_norm):
    h1 = h_ref[...] + _dot(o_ref[...], wo_ref[...])
    hn = _rms(h1, g_ref[...], EPS).astype(BF16)
    acc = h1
    for f in range(D_FF // FF_CHUNK):
        cols = slice(f * FF_CHUNK, (f + 1) * FF_CHUNK)
        u = jnp.maximum(_dot(hn, wup_ref[:, cols]), 0.0)
        acc = acc + _dot((u * u).astype(BF16), wdn_ref[cols, :])
    if final_norm:
        acc = _rms(acc, gf_ref[...], EPS)
    out_ref[...] = acc


def _post(h2, o2, w_o, g_mlp, w_up, w_down, g_final, final_norm, name):
    t = h2.shape[0]
    tm = TOKEN_TILE
    const = lambda i: (0, 0)
    row = lambda i: (i, 0)
    resident = functools.partial(pl.BlockSpec, index_map=const, pipeline_mode=pl.Buffered(1))
    return pl.pallas_call(
        functools.partial(_post_kernel, final_norm=final_norm),
        grid=(t // tm,),
        in_specs=[
            pl.BlockSpec((tm, D_MODEL), row),
            pl.BlockSpec((tm, D_MODEL), row),
            resident((D_MODEL, D_MODEL)),
            pl.BlockSpec((1, D_MODEL), const),
            resident((D_MODEL, D_FF)),
            resident((D_FF, D_MODEL)),
            pl.BlockSpec((1, D_MODEL), const),
        ],
        out_specs=pl.BlockSpec((tm, D_MODEL), row),
        out_shape=jax.ShapeDtypeStruct((t, D_MODEL), F32),
        compiler_params=pltpu.CompilerParams(dimension_semantics=("arbitrary",), vmem_limit_bytes=VMEM_LIMIT),
        name=name,
    )(h2, o2, w_o, g_mlp, w_up, w_down, g_final)


def _proj_b_kernel(h_ref, gkv_ref, wdkv_ref, gc_ref, wuk_ref, wuv_ref, gq_ref, wdq_ref, gcq_ref, wuq_ref,
                   cos_ref, sin_ref, q_ref, k_ref, vt_ref):
    h = h_ref[...]
    cos, sin = cos_ref[...], sin_ref[...]
    ckv = _dot(_rms(h, gkv_ref[...], EPS).astype(BF16), wdkv_ref[...])
    c = _rms(ckv[:, :KV_LORA], gc_ref[...], EPS).astype(BF16)
    k_rope = _rope_slab(ckv[:, KV_LORA:], cos, sin).astype(BF16)
    k_nope = _dot(c, wuk_ref[...]).astype(BF16)
    for hd in range(HEADS):
        k_ref[:, hd * SLOT:hd * SLOT + LANES] = k_nope[:, hd * LANES:(hd + 1) * LANES]
        k_ref[:, hd * SLOT + LANES:(hd + 1) * SLOT] = k_rope
    v = _dot(c, wuv_ref[...])
    for hd in range(HEADS):
        vt_ref[0, hd, 0] = v[:, hd * V_DIM:(hd + 1) * V_DIM].T.astype(BF16)
    cq = _rms(_dot(_rms(h, gq_ref[...], EPS).astype(BF16), wdq_ref[...]), gcq_ref[...], EPS).astype(BF16)
    q = _dot(cq, wuq_ref[...])
    scale = (MLA_NOPE + MLA_ROPE) ** -0.5 * LOG2_E
    for hd in range(HEADS):
        q_ref[:, hd * SLOT:hd * SLOT + LANES] = (q[:, hd * SLOT:hd * SLOT + LANES] * scale).astype(BF16)
        q_rope = _rope_slab(q[:, hd * SLOT + LANES:(hd + 1) * SLOT], cos, sin)
        q_ref[:, hd * SLOT + LANES:(hd + 1) * SLOT] = (q_rope * scale).astype(BF16)


def _proj_b(h2, g_kv_in, w_dkv, g_c, w_uk, w_uv, g_q, w_dq, g_cq, w_uq, cos, sin, seq):
    t = h2.shape[0]
    tm = TOKEN_TILE
    pos_blocks = seq // tm
    const = lambda i: (0, 0)
    row = lambda i: (i, 0)
    full = lambda a: pl.BlockSpec(a.shape, const)
    return pl.pallas_call(
        _proj_b_kernel,
        grid=(t // tm,),
        in_specs=[
            pl.BlockSpec((tm, D_MODEL), row),
            full(g_kv_in), full(w_dkv), full(g_c), full(w_uk), full(w_uv),
            full(g_q), full(w_dq), full(g_cq), full(w_uq),
            pl.BlockSpec((tm, LANES), lambda i: (i % pos_blocks, 0)),
            pl.BlockSpec((tm, LANES), lambda i: (i % pos_blocks, 0)),
        ],
        out_specs=[
            pl.BlockSpec((tm, HEADS * SLOT), row),
            pl.BlockSpec((tm, HEADS * SLOT), row),
            _values_t_spec(pos_blocks),
        ],
        out_shape=[
            jax.ShapeDtypeStruct((t, HEADS * SLOT), BF16),
            jax.ShapeDtypeStruct((t, HEADS * SLOT), BF16),
            _values_t_shape(t, seq),
        ],
        compiler_params=pltpu.CompilerParams(dimension_semantics=("arbitrary",), vmem_limit_bytes=VMEM_LIMIT),
        name="proj_b",
    )(h2, g_kv_in, w_dkv, g_c, w_uk, w_uv, g_q, w_dq, g_cq, w_uq, cos, sin)


def _rope_tables(seq):
    pos = jnp.arange(seq, dtype=F32)
    inv_freq = ROPE_THETA ** (-jnp.arange(0, HEAD_DIM, 2, dtype=F32) / HEAD_DIM)
    ang = pos[:, None] * inv_freq[None, :]
    cos, sin = jnp.cos(ang), jnp.sin(ang)
    return jnp.tile(cos, (1, 4)), jnp.tile(jnp.concatenate([-sin, sin], axis=1), (1, 2))


def _pad_cols(w, width):
    return jnp.pad(w, ((0, 0), (0, width - w.shape[1])))


def kernel(x, attn_norm_g, w_qkv_a, lambda_q1, lambda_k1, lambda_q2, lambda_k2, subln_g, w_o_a, kv_in_norm_g, w_dkv, kv_norm_g, w_ukv, w_dq, q_norm_g, w_uq, w_o_b, mlp_norm_g, w_up, w_down, final_norm_g):
    b, s, _ = x.shape
    t = b * s
    cos, sin = _rope_tables(s)
    row = lambda g: g.reshape(1, -1)
    h = x.reshape(t, D_MODEL)

    lambda_init = 0.8 - 0.6 * math.exp(-0.3 * 0)
    qz, k, vt = _proj_a(h, row(attn_norm_g[0]), w_qkv_a[0].astype(BF16), cos, sin, s)
    lam_rows = [row(p[0]) for p in (lambda_q1, lambda_k1, lambda_q2, lambda_k2)]
    o = _attn_call(functools.partial(_diff_attn_kernel, lambda_init=lambda_init), 2,
                   qz.reshape(b, s, HEADS * SLOT), k.reshape(b, s, D_MODEL), LANES,
                   vt, [subln_g[0].reshape(V_DIM, 1)] + lam_rows, "diff_attn")
    h = _post(h, o.reshape(t, D_MODEL), w_o_a[0].astype(BF16), row(mlp_norm_g[0]),
              w_up[0].astype(BF16), w_down[0].astype(BF16), row(final_norm_g), False, "post_a")

    w_ukv_h = w_ukv.reshape(KV_LORA, HEADS, MLA_NOPE + V_DIM)
    w_uk = w_ukv_h[:, :, :MLA_NOPE].reshape(KV_LORA, HEADS * MLA_NOPE)
    w_uv = w_ukv_h[:, :, MLA_NOPE:].reshape(KV_LORA, HEADS * V_DIM)
    w_uq_slots = jnp.pad(w_uq[0].reshape(Q_LORA, HEADS, MLA_NOPE + MLA_ROPE),
                         ((0, 0), (0, 0), (0, SLOT - MLA_NOPE - MLA_ROPE))).reshape(Q_LORA, HEADS * SLOT)
    q, kk, vt = _proj_b(h, row(kv_in_norm_g), _pad_cols(w_dkv, KV_LORA + LANES).astype(BF16), row(kv_norm_g),
                       w_uk.astype(BF16), w_uv.astype(BF16), row(attn_norm_g[1]), w_dq[0].astype(BF16),
                       row(q_norm_g[0]), w_uq_slots.astype(BF16), cos, sin, s)
    o = _attn_call(_mla_attn_kernel, 1, q.reshape(b, s, HEADS * SLOT), kk.reshape(b, s, HEADS * SLOT), SLOT,
                   vt, [], "mla_attn")
    h = _post(h, o.reshape(t, D_MODEL), w_o_b[0].astype(BF16), row(mlp_norm_g[1]),
              w_up[1].astype(BF16), w_down[1].astype(BF16), row(final_norm_g), True, "post_b")
    return h.reshape(b, s, D_MODEL)
```

```python
import functools
import math

import jax
import jax.numpy as jnp
from jax import lax
from jax.experimental import pallas as pl
from jax.experimental.pallas import tpu as pltpu

D_MODEL = 1024
DEPTH = 2
N_A = DEPTH // 2
HEADS = 8
HEAD_DIM = 64
V_DIM = 128
MLA_NOPE = 128
MLA_ROPE = 64
Q_LORA = 384
KV_LORA = 256
D_FF = 4 * D_MODEL
ROPE_THETA = 10000.0
EPS = 1e-6
SUBLN_EPS = 1e-5

LANES = 128
SLOT = 2 * LANES
TOKEN_TILE = 512
Q_TILE = 256
K_STEP = 512
Q_SUB = 8
STEPS_PER_SUPER = Q_SUB * Q_TILE // K_STEP
ONES_ROWS = 16
FF_CHUNK = 1024
NEG_BIG = -1e30
LOG2_E = math.log2(math.e)
VMEM_LIMIT = 56 * 1024 * 1024

BF16 = jnp.bfloat16
F32 = jnp.float32
NT_DIMS = (((1,), (1,)), ((), ()))


def _rms(x, g, eps):
    return x * lax.rsqrt(jnp.mean(x * x, axis=-1, keepdims=True) + eps) * g


def _rope_slab(x, cos, sin_signed):
    lane = lax.broadcasted_iota(jnp.int32, x.shape, 1)
    first = (lane % HEAD_DIM) < (HEAD_DIM // 2)
    swapped = jnp.where(first, pltpu.roll(x, LANES - HEAD_DIM // 2, 1), pltpu.roll(x, HEAD_DIM // 2, 1))
    return x * cos + swapped * sin_signed


def _dot(a, b):
    return jnp.dot(a, b, preferred_element_type=F32)


def _values_t_spec(steps_per_seq):
    assert TOKEN_TILE == K_STEP
    return pl.BlockSpec((1, HEADS, 1, V_DIM, K_STEP), lambda i: (i // steps_per_seq, 0, i % steps_per_seq, 0, 0))


def _values_t_shape(tokens, seq):
    return jax.ShapeDtypeStruct((tokens // seq, HEADS, seq // K_STEP, V_DIM, K_STEP), BF16)


def _proj_a_kernel(x_ref, g_ref, w_ref, cos_ref, sin_ref, qz_ref, k_ref, vt_ref):
    hn = _rms(x_ref[...], g_ref[...], EPS).astype(BF16)
    qkv = _dot(hn, w_ref[...])
    cos, sin = cos_ref[...], sin_ref[...]
    lane = lax.broadcasted_iota(jnp.int32, (x_ref.shape[0], LANES), 1)
    comp1 = lane < HEAD_DIM
    scale = HEAD_DIM ** -0.5 * LOG2_E
    for h in range(HEADS):
        q = _rope_slab(qkv[:, h * LANES:(h + 1) * LANES], cos, sin) * scale
        qz_ref[:, h * SLOT:h * SLOT + LANES] = jnp.where(comp1, q, 0.0).astype(BF16)
        qz_ref[:, h * SLOT + LANES:(h + 1) * SLOT] = jnp.where(comp1, 0.0, q).astype(BF16)
        k = _rope_slab(qkv[:, D_MODEL + h * LANES:D_MODEL + (h + 1) * LANES], cos, sin)
        k_ref[:, h * LANES:(h + 1) * LANES] = k.astype(BF16)
        v = qkv[:, 2 * D_MODEL + h * V_DIM:2 * D_MODEL + (h + 1) * V_DIM]
        vt_ref[0, h, 0] = v.T.astype(BF16)


def _proj_a(x2, g, w_qkv, cos, sin, seq):
    t = x2.shape[0]
    tm = TOKEN_TILE
    pos_blocks = seq // tm
    const = lambda i: (0, 0)
    return pl.pallas_call(
        _proj_a_kernel,
        grid=(t // tm,),
        in_specs=[
            pl.BlockSpec((tm, D_MODEL), lambda i: (i, 0)),
            pl.BlockSpec((1, D_MODEL), const),
            pl.BlockSpec((D_MODEL, 3 * D_MODEL), const),
            pl.BlockSpec((tm, LANES), lambda i: (i % pos_blocks, 0)),
            pl.BlockSpec((tm, LANES), lambda i: (i % pos_blocks, 0)),
        ],
        out_specs=[
            pl.BlockSpec((tm, HEADS * SLOT), lambda i: (i, 0)),
            pl.BlockSpec((tm, D_MODEL), lambda i: (i, 0)),
            _values_t_spec(pos_blocks),
        ],
        out_shape=[
            jax.ShapeDtypeStruct((t, HEADS * SLOT), BF16),
            jax.ShapeDtypeStruct((t, D_MODEL), BF16),
            _values_t_shape(t, seq),
        ],
        compiler_params=pltpu.CompilerParams(dimension_semantics=("arbitrary",), vmem_limit_bytes=VMEM_LIMIT),
        name="proj_a",
    )(x2, g, w_qkv, cos, sin)


def _causal_keep(offset, n_keys):
    key = lax.broadcasted_iota(jnp.int32, (n_keys, Q_TILE), 0)
    qry = lax.broadcasted_iota(jnp.int32, (n_keys, Q_TILE), 1)
    return key <= qry + offset


def _diag_plan(e):
    plan = []
    for a in range(Q_SUB):
        offset = Q_TILE * a - K_STEP * e
        if offset + Q_TILE > 0:
            plan.append((a, None if offset >= K_STEP - 1 else offset, min(K_STEP, offset + Q_TILE)))
    return plan


def _n_softmax(q_cols):
    return 1 if q_cols is None else len(q_cols)


def _score_chain(q_ref, q_cols, k_ref, s_ref, mx_ref, sup, step, a, c, slot, n_keys=K_STEP):
    k_t = k_ref[0, pl.ds(pl.multiple_of(step * K_STEP, K_STEP), n_keys), :]
    if q_cols is None:
        s_t = _dot(k_t, q_ref[0, 0, sup * Q_SUB + a])
    else:
        rows = pl.ds(pl.multiple_of((sup * Q_SUB + a) * Q_TILE, Q_TILE), Q_TILE)
        s_t = lax.dot_general(k_t, q_ref[0, rows, q_cols[c]], NT_DIMS, preferred_element_type=F32)
    ch = a * _n_softmax(q_cols) + c
    s_ref[slot, ch, :n_keys] = s_t
    mx_ref[slot, ch] = jnp.max(s_t, axis=0, keepdims=True)


def _flash_scores(q_ref, q_cols, k_ref, s_ref, mx_ref, sup, step, plan, slot):
    for a, _, n_keys in plan:
        for c in range(_n_softmax(q_cols)):
            _score_chain(q_ref, q_cols, k_ref, s_ref, mx_ref, sup, step, a, c, slot, n_keys)


EVERYONE = [(a, None, K_STEP) for a in range(Q_SUB)]


def _flash_super_tile(q_ref, q_cols, k_ref, vt_ref, s_ref, mx_ref, m_ref, acc_ref, sup, prefetch_next, finalize):
    n = _n_softmax(q_cols)
    m_ref[...] = jnp.full(m_ref.shape, NEG_BIG, F32)
    acc_ref[...] = jnp.zeros(acc_ref.shape, F32)
    first_step = sup * STEPS_PER_SUPER
    ones = jnp.ones((ONES_ROWS, K_STEP), BF16)

    def scores(step, plan, slot):
        _flash_scores(q_ref, q_cols, k_ref, s_ref, mx_ref, sup, step, plan, slot)

    def update(step, plan, slot, after_chain=None, before_chain=None):
        vt_aug = jnp.concatenate([vt_ref[0, 0, step], ones], axis=0)
        for a, offset, n_keys in plan:
            for c in range(n):
                ch = a * n + c
                if before_chain is not None:
                    before_chain(a, c)
                if offset is None:
                    s_t, mx = s_ref[slot, ch], mx_ref[slot, ch]
                else:
                    s_t = jnp.where(_causal_keep(offset, n_keys), s_ref[slot, ch, :n_keys], NEG_BIG)
                    mx = jnp.max(s_t, axis=0, keepdims=True)
                m_old = m_ref[ch]
                m_new = jnp.maximum(m_old, mx)
                p = jnp.exp2(s_t - m_new).astype(BF16)
                acc_ref[ch] = jnp.exp2(m_old - m_new) * acc_ref[ch] + _dot(vt_aug[:, :n_keys], p)
                m_ref[ch] = m_new
                if after_chain is not None:
                    after_chain(a, c)

    def pair_body(jj, carry):
        step = 2 * jj
        update(step, EVERYONE, 0, before_chain=functools.partial(
            _score_chain, q_ref, q_cols, k_ref, s_ref, mx_ref, sup, step + 1, slot=1))
        update(step + 1, EVERYONE, 1, before_chain=functools.partial(
            _score_chain, q_ref, q_cols, k_ref, s_ref, mx_ref, sup, step + 2, slot=0))
        return carry

    lax.fori_loop(0, sup * (STEPS_PER_SUPER // 2), pair_body, 0)
    for e in range(STEPS_PER_SUPER):
        plan = _diag_plan(e)[::-1]
        ahead = None
        if e + 1 < STEPS_PER_SUPER:
            keys_next = {a: n_keys for a, _, n_keys in _diag_plan(e + 1)}

            def ahead(a, c, e=e, keys_next=keys_next):
                if a in keys_next:
                    _score_chain(q_ref, q_cols, k_ref, s_ref, mx_ref, sup, first_step + e + 1, a, c, (e + 1) % 2,
                                 keys_next[a])
        refill = None
        if prefetch_next and e == STEPS_PER_SUPER - 2:
            assert e % 2 == 0
            refill = functools.partial(_score_chain, q_ref, q_cols, k_ref, s_ref, mx_ref, sup + 1, 0, slot=0)
            for a in range(Q_SUB):
                if a not in [b for b, _, _ in plan]:
                    for c in range(n):
                        refill(a, c)
        update(first_step + e, plan, e % 2, refill, ahead)
        for a in range(Q_SUB):
            if (Q_TILE * a + Q_TILE - 1) // K_STEP == e:
                finalize(a)


def _run_super_tiles(super_tile, q_ref, q_cols, k_ref, s_ref, mx_ref, n_super):
    _flash_scores(q_ref, q_cols, k_ref, s_ref, mx_ref, 0, 0, _diag_plan(0), 0)

    def body(sup, carry):
        super_tile(sup, True)
        return carry

    lax.fori_loop(0, n_super - 1, body, 0)
    super_tile(n_super - 1, False)


def _softmax_out(acc_ref, ch):
    acc = acc_ref[ch]
    return acc[:V_DIM] * (1.0 / acc[V_DIM:V_DIM + 1])


def _diff_attn_kernel(qz_ref, k_ref, vt_ref, g_ref, lq1_ref, lk1_ref, lq2_ref, lk2_ref, o_ref,
                      s_ref, mx_ref, m_ref, acc_ref, *, lambda_init):
    lam = (jnp.exp(jnp.sum(lq1_ref[...] * lk1_ref[...], keepdims=True))
           - jnp.exp(jnp.sum(lq2_ref[...] * lk2_ref[...], keepdims=True)) + lambda_init)
    g_col = g_ref[...]
    q_cols = [slice(0, LANES), slice(LANES, SLOT)]

    def super_tile(sup, prefetch_next):
        def finalize(a):
            o_t = _softmax_out(acc_ref, 2 * a) - lam * _softmax_out(acc_ref, 2 * a + 1)
            y_t = o_t * lax.rsqrt(jnp.mean(o_t * o_t, axis=0, keepdims=True) + SUBLN_EPS) * g_col
            rows = pl.ds(pl.multiple_of((sup * Q_SUB + a) * Q_TILE, Q_TILE), Q_TILE)
            o_ref[0, rows, :] = (y_t * (1.0 - lambda_init)).T.astype(BF16)

        _flash_super_tile(qz_ref, q_cols, k_ref, vt_ref, s_ref, mx_ref, m_ref, acc_ref, sup, prefetch_next, finalize)

    _run_super_tiles(super_tile, qz_ref, q_cols, k_ref, s_ref, mx_ref, o_ref.shape[1] // (Q_TILE * Q_SUB))


def _mla_attn_kernel(q_ref, k_ref, vt_ref, o_ref, s_ref, mx_ref, m_ref, acc_ref):
    q_cols = None

    def super_tile(sup, prefetch_next):
        def finalize(a):
            rows = pl.ds(pl.multiple_of((sup * Q_SUB + a) * Q_TILE, Q_TILE), Q_TILE)
            o_ref[0, rows, :] = _softmax_out(acc_ref, a).T.astype(BF16)

        _flash_super_tile(q_ref, q_cols, k_ref, vt_ref, s_ref, mx_ref, m_ref, acc_ref, sup, prefetch_next, finalize)

    _run_super_tiles(super_tile, q_ref, q_cols, k_ref, s_ref, mx_ref, o_ref.shape[1] // (Q_TILE * Q_SUB))


def _attn_call(kernel, n_softmax, q, k, k_width, vt, extra, name):
    b, s, _ = k.shape
    chains = n_softmax * Q_SUB
    small = lambda bi, hi: (0, 0)
    if q.ndim == 3:
        q_spec = pl.BlockSpec((1, s, SLOT), lambda bi, hi: (bi, 0, hi))
    else:
        q_spec = pl.BlockSpec((1, 1) + q.shape[2:], lambda bi, hi: (bi, hi, 0, 0, 0))
    in_specs = [
        q_spec,
        pl.BlockSpec((1, s, k_width), lambda bi, hi: (bi, 0, hi)),
        pl.BlockSpec((1, 1, s // K_STEP, V_DIM, K_STEP), lambda bi, hi: (bi, hi, 0, 0, 0)),
    ] + [pl.BlockSpec(e.shape, small) for e in extra]
    return pl.pallas_call(
        kernel,
        grid=(b, HEADS),
        in_specs=in_specs,
        out_specs=pl.BlockSpec((1, s, V_DIM), lambda bi, hi: (bi, 0, hi)),
        out_shape=jax.ShapeDtypeStruct((b, s, HEADS * V_DIM), BF16),
        scratch_shapes=[
            pltpu.VMEM((2, chains, K_STEP, Q_TILE), F32),
            pltpu.VMEM((2, chains, 1, Q_TILE), F32),
            pltpu.VMEM((chains, 1, Q_TILE), F32),
            pltpu.VMEM((chains, V_DIM + ONES_ROWS, Q_TILE), F32),
        ],
        compiler_params=pltpu.CompilerParams(dimension_semantics=("arbitrary", "arbitrary"),
                                             vmem_limit_bytes=VMEM_LIMIT),
        name=name,
    )(q, k, vt, *extra)


def _post_kernel(h_ref, o_ref, wo_ref, g_ref, wup_ref, wdn_ref, gf_ref, out_ref, *, final_norm):
    h1 = h_ref[...] + _dot(o_ref[...], wo_ref[...])
    hn = _rms(h1, g_ref[...], EPS).astype(BF16)
    acc = h1
    for f in range(D_FF // FF_CHUNK):
        cols = slice(f * FF_CHUNK, (f + 1) * FF_CHUNK)
        u = jnp.maximum(_dot(hn, wup_ref[:, cols]), 0.0)
        acc = acc + _dot((u * u).astype(BF16), wdn_ref[cols, :])
    if final_norm:
        acc = _rms(acc, gf_ref[...], EPS)
    out_ref[...] = acc


def _post(h2, o2, w_o, g_mlp, w_up, w_down, g_final, final_norm, name):
    t = h2.shape[0]
    tm = TOKEN_TILE
    const = lambda i: (0, 0)
    row = lambda i: (i, 0)
    resident = functools.partial(pl.BlockSpec, index_map=const, pipeline_mode=pl.Buffered(1))
    return pl.pallas_call(
        functools.partial(_post_kernel, final_norm=final_norm),
        grid=(t // tm,),
        in_specs=[
            pl.BlockSpec((tm, D_MODEL), row),
            pl.BlockSpec((tm, D_MODEL), row),
            resident((D_MODEL, D_MODEL)),
            pl.BlockSpec((1, D_MODEL), const),
            resident((D_MODEL, D_FF)),
            resident((D_FF, D_MODEL)),
            pl.BlockSpec((1, D_MODEL), const),
        ],
        out_specs=pl.BlockSpec((tm, D_MODEL), row),
        out_shape=jax.ShapeDtypeStruct((t, D_MODEL), F32),
        compiler_params=pltpu.CompilerParams(dimension_semantics=("arbitrary",), vmem_limit_bytes=VMEM_LIMIT),
        name=name,
    )(h2, o2, w_o, g_mlp, w_up, w_down, g_final)


def _proj_b_kernel(h_ref, gkv_ref, wdkv_ref, gc_ref, wuk_ref, wuv_ref, gq_ref, wdq_ref, gcq_ref, wuqt_ref,
                   cos_ref, sin_ref, cost_ref, sint_ref, q_ref, k_ref, vt_ref):
    h = h_ref[...]
    cos, sin = cos_ref[...], sin_ref[...]
    ckv = _dot(_rms(h, gkv_ref[...], EPS).astype(BF16), wdkv_ref[...])
    c = _rms(ckv[:, :KV_LORA], gc_ref[...], EPS).astype(BF16)
    k_rope = _rope_slab(ckv[:, KV_LORA:], cos, sin).astype(BF16)
    k_nope = _dot(c, wuk_ref[...]).astype(BF16)
    for hd in range(HEADS):
        k_ref[:, hd * SLOT:hd * SLOT + LANES] = k_nope[:, hd * LANES:(hd + 1) * LANES]
        k_ref[:, hd * SLOT + LANES:(hd + 1) * SLOT] = k_rope
    v = _dot(c, wuv_ref[...])
    for hd in range(HEADS):
        vt_ref[0, hd, 0] = v[:, hd * V_DIM:(hd + 1) * V_DIM].T.astype(BF16)
    cq = _rms(_dot(_rms(h, gq_ref[...], EPS).astype(BF16), wdq_ref[...]), gcq_ref[...], EPS)
    q_t = _dot(wuqt_ref[...], cq.T.astype(BF16))
    cos_t, sin_t = cost_ref[...], sint_ref[...]
    scale = (MLA_NOPE + MLA_ROPE) ** -0.5 * LOG2_E
    half = MLA_ROPE // 2
    pad = jnp.zeros((SLOT - MLA_NOPE - MLA_ROPE, q_t.shape[1]), F32)
    for hd in range(HEADS):
        base = hd * (MLA_NOPE + MLA_ROPE)
        x1 = q_t[base + MLA_NOPE:base + MLA_NOPE + half]
        x2 = q_t[base + MLA_NOPE + half:base + MLA_NOPE + MLA_ROPE]
        tile = jnp.concatenate([q_t[base:base + MLA_NOPE], x1 * cos_t - x2 * sin_t, x2 * cos_t + x1 * sin_t, pad],
                               axis=0) * scale
        for j in range(q_t.shape[1] // Q_TILE):
            q_ref[0, hd, j] = tile[:, j * Q_TILE:(j + 1) * Q_TILE].astype(BF16)


def _proj_b(h2, g_kv_in, w_dkv, g_c, w_uk, w_uv, g_q, w_dq, g_cq, w_uq_t, cos, sin, cos_t, sin_t, seq):
    t = h2.shape[0]
    tm = TOKEN_TILE
    pos_blocks = seq // tm
    const = lambda i: (0, 0)
    row = lambda i: (i, 0)
    full = lambda a: pl.BlockSpec(a.shape, const)
    return pl.pallas_call(
        _proj_b_kernel,
        grid=(t // tm,),
        in_specs=[
            pl.BlockSpec((tm, D_MODEL), row),
            full(g_kv_in), full(w_dkv), full(g_c), full(w_uk), full(w_uv),
            full(g_q), full(w_dq), full(g_cq), full(w_uq_t),
            pl.BlockSpec((tm, LANES), lambda i: (i % pos_blocks, 0)),
            pl.BlockSpec((tm, LANES), lambda i: (i % pos_blocks, 0)),
            pl.BlockSpec((MLA_ROPE // 2, tm), lambda i: (0, i % pos_blocks)),
            pl.BlockSpec((MLA_ROPE // 2, tm), lambda i: (0, i % pos_blocks)),
        ],
        out_specs=[
            pl.BlockSpec((1, HEADS, tm // Q_TILE, SLOT, Q_TILE), lambda i: (i // pos_blocks, 0, i % pos_blocks, 0, 0)),
            pl.BlockSpec((tm, HEADS * SLOT), row),
            _values_t_spec(pos_blocks),
        ],
        out_shape=[
            jax.ShapeDtypeStruct((t // seq, HEADS, seq // Q_TILE, SLOT, Q_TILE), BF16),
            jax.ShapeDtypeStruct((t, HEADS * SLOT), BF16),
            _values_t_shape(t, seq),
        ],
        compiler_params=pltpu.CompilerParams(dimension_semantics=("arbitrary",), vmem_limit_bytes=VMEM_LIMIT),
        name="proj_b",
    )(h2, g_kv_in, w_dkv, g_c, w_uk, w_uv, g_q, w_dq, g_cq, w_uq_t, cos, sin, cos_t, sin_t)


def _rope_tables(seq):
    pos = jnp.arange(seq, dtype=F32)
    inv_freq = ROPE_THETA ** (-jnp.arange(0, HEAD_DIM, 2, dtype=F32) / HEAD_DIM)
    ang = pos[:, None] * inv_freq[None, :]
    cos, sin = jnp.cos(ang), jnp.sin(ang)
    return jnp.tile(cos, (1, 4)), jnp.tile(jnp.concatenate([-sin, sin], axis=1), (1, 2)), cos.T, sin.T


def _pad_cols(w, width):
    return jnp.pad(w, ((0, 0), (0, width - w.shape[1])))


def kernel(x, attn_norm_g, w_qkv_a, lambda_q1, lambda_k1, lambda_q2, lambda_k2, subln_g, w_o_a, kv_in_norm_g, w_dkv, kv_norm_g, w_ukv, w_dq, q_norm_g, w_uq, w_o_b, mlp_norm_g, w_up, w_down, final_norm_g):
    b, s, _ = x.shape
    t = b * s
    cos, sin, cos_t, sin_t = _rope_tables(s)
    row = lambda g: g.reshape(1, -1)
    h = x.reshape(t, D_MODEL)

    lambda_init = 0.8 - 0.6 * math.exp(-0.3 * 0)
    qz, k, vt = _proj_a(h, row(attn_norm_g[0]), w_qkv_a[0].astype(BF16), cos, sin, s)
    lam_rows = [row(p[0]) for p in (lambda_q1, lambda_k1, lambda_q2, lambda_k2)]
    o = _attn_call(functools.partial(_diff_attn_kernel, lambda_init=lambda_init), 2,
                   qz.reshape(b, s, HEADS * SLOT), k.reshape(b, s, D_MODEL), LANES,
                   vt, [subln_g[0].reshape(V_DIM, 1)] + lam_rows, "diff_attn")
    h = _post(h, o.reshape(t, D_MODEL), w_o_a[0].astype(BF16), row(mlp_norm_g[0]),
              w_up[0].astype(BF16), w_down[0].astype(BF16), row(final_norm_g), False, "post_a")

    w_ukv_h = w_ukv.reshape(KV_LORA, HEADS, MLA_NOPE + V_DIM)
    w_uk = w_ukv_h[:, :, :MLA_NOPE].reshape(KV_LORA, HEADS * MLA_NOPE)
    w_uv = w_ukv_h[:, :, MLA_NOPE:].reshape(KV_LORA, HEADS * V_DIM)
    q, kk, vt = _proj_b(h, row(kv_in_norm_g), _pad_cols(w_dkv, KV_LORA + LANES).astype(BF16), row(kv_norm_g),
                       w_uk.astype(BF16), w_uv.astype(BF16), row(attn_norm_g[1]), w_dq[0].astype(BF16),
                       row(q_norm_g[0]), w_uq[0].T.astype(BF16), cos, sin, cos_t, sin_t, s)
    o = _attn_call(_mla_attn_kernel, 1, q, kk.reshape(b, s, HEADS * SLOT), SLOT,
                   vt, [], "mla_attn")
    h = _post(h, o.reshape(t, D_MODEL), w_o_b[0].astype(BF16), row(mlp_norm_g[1]),
              w_up[1].astype(BF16), w_down[1].astype(BF16), row(final_norm_g), True, "post_b")
    return h.reshape(b, s, D_MODEL)
```

```python
import functools
import math

import jax
import jax.numpy as jnp
from jax import lax
from jax.experimental import pallas as pl
from jax.experimental.pallas import tpu as pltpu

D_MODEL = 1024
DEPTH = 2
N_A = DEPTH // 2
HEADS = 8
HEAD_DIM = 64
V_DIM = 128
MLA_NOPE = 128
MLA_ROPE = 64
Q_LORA = 384
KV_LORA = 256
D_FF = 4 * D_MODEL
ROPE_THETA = 10000.0
EPS = 1e-6
SUBLN_EPS = 1e-5

LANES = 128
SLOT = 2 * LANES
TOKEN_TILE = 512
MLP_TILE = 1024
Q_TILE = 256
K_STEP = 512
Q_SUB = 8
STEPS_PER_SUPER = Q_SUB * Q_TILE // K_STEP
ONES_ROWS = 16
FF_CHUNK = 1024
NEG_BIG = -1e30
LOG2_E = math.log2(math.e)
VMEM_LIMIT = 56 * 1024 * 1024

BF16 = jnp.bfloat16
F32 = jnp.float32
NT_DIMS = (((1,), (1,)), ((), ()))


def _rms(x, g, eps):
    return x * lax.rsqrt(jnp.mean(x * x, axis=-1, keepdims=True) + eps) * g


def _rope_slab(x, cos, sin_signed):
    lane = lax.broadcasted_iota(jnp.int32, x.shape, 1)
    first = (lane % HEAD_DIM) < (HEAD_DIM // 2)
    swapped = jnp.where(first, pltpu.roll(x, LANES - HEAD_DIM // 2, 1), pltpu.roll(x, HEAD_DIM // 2, 1))
    return x * cos + swapped * sin_signed


def _dot(a, b):
    return jnp.dot(a, b, preferred_element_type=F32)


def _values_t_spec(steps_per_seq):
    assert TOKEN_TILE == K_STEP
    return pl.BlockSpec((1, HEADS, 1, V_DIM, K_STEP), lambda i: (i // steps_per_seq, 0, i % steps_per_seq, 0, 0))


def _values_t_shape(tokens, seq):
    return jax.ShapeDtypeStruct((tokens // seq, HEADS, seq // K_STEP, V_DIM, K_STEP), BF16)


def _proj_a_kernel(x_ref, g_ref, w_ref, cos_ref, sin_ref, qz_ref, k_ref, vt_ref):
    hn = _rms(x_ref[...], g_ref[...], EPS).astype(BF16)
    qkv = _dot(hn, w_ref[...])
    cos, sin = cos_ref[...], sin_ref[...]
    lane = lax.broadcasted_iota(jnp.int32, (x_ref.shape[0], LANES), 1)
    comp1 = lane < HEAD_DIM
    scale = HEAD_DIM ** -0.5 * LOG2_E
    for h in range(HEADS):
        q = _rope_slab(qkv[:, h * LANES:(h + 1) * LANES], cos, sin) * scale
        qz_ref[:, h * SLOT:h * SLOT + LANES] = jnp.where(comp1, q, 0.0).astype(BF16)
        qz_ref[:, h * SLOT + LANES:(h + 1) * SLOT] = jnp.where(comp1, 0.0, q).astype(BF16)
        k = _rope_slab(qkv[:, D_MODEL + h * LANES:D_MODEL + (h + 1) * LANES], cos, sin)
        k_ref[:, h * LANES:(h + 1) * LANES] = k.astype(BF16)
        v = qkv[:, 2 * D_MODEL + h * V_DIM:2 * D_MODEL + (h + 1) * V_DIM]
        vt_ref[0, h, 0] = v.T.astype(BF16)


def _proj_a(x2, g, w_qkv, cos, sin, seq):
    t = x2.shape[0]
    tm = TOKEN_TILE
    pos_blocks = seq // tm
    const = lambda i: (0, 0)
    return pl.pallas_call(
        _proj_a_kernel,
        grid=(t // tm,),
        in_specs=[
            pl.BlockSpec((tm, D_MODEL), lambda i: (i, 0)),
            pl.BlockSpec((1, D_MODEL), const),
            pl.BlockSpec((D_MODEL, 3 * D_MODEL), const),
            pl.BlockSpec((tm, LANES), lambda i: (i % pos_blocks, 0)),
            pl.BlockSpec((tm, LANES), lambda i: (i % pos_blocks, 0)),
        ],
        out_specs=[
            pl.BlockSpec((tm, HEADS * SLOT), lambda i: (i, 0)),
            pl.BlockSpec((tm, D_MODEL), lambda i: (i, 0)),
            _values_t_spec(pos_blocks),
        ],
        out_shape=[
            jax.ShapeDtypeStruct((t, HEADS * SLOT), BF16),
            jax.ShapeDtypeStruct((t, D_MODEL), BF16),
            _values_t_shape(t, seq),
        ],
        compiler_params=pltpu.CompilerParams(dimension_semantics=("arbitrary",), vmem_limit_bytes=VMEM_LIMIT),
        name="proj_a",
    )(x2, g, w_qkv, cos, sin)


def _causal_keep(offset, n_keys):
    key = lax.broadcasted_iota(jnp.int32, (n_keys, Q_TILE), 0)
    qry = lax.broadcasted_iota(jnp.int32, (n_keys, Q_TILE), 1)
    return key <= qry + offset


def _diag_plan(e):
    plan = []
    for a in range(Q_SUB):
        offset = Q_TILE * a - K_STEP * e
        if offset + Q_TILE > 0:
            plan.append((a, None if offset >= K_STEP - 1 else offset, min(K_STEP, offset + Q_TILE)))
    return plan


def _n_softmax(q_cols):
    return 1 if q_cols is None else len(q_cols)


def _score_chain(q_ref, q_cols, k_ref, s_ref, mx_ref, sup, step, a, c, slot, n_keys=K_STEP):
    k_t = k_ref[0, pl.ds(pl.multiple_of(step * K_STEP, K_STEP), n_keys), :]
    if q_cols is None:
        s_t = _dot(k_t, q_ref[0, 0, sup * Q_SUB + a])
    else:
        rows = pl.ds(pl.multiple_of((sup * Q_SUB + a) * Q_TILE, Q_TILE), Q_TILE)
        s_t = lax.dot_general(k_t, q_ref[0, rows, q_cols[c]], NT_DIMS, preferred_element_type=F32)
    ch = a * _n_softmax(q_cols) + c
    s_ref[slot, ch, :n_keys] = s_t
    mx_ref[slot, ch] = jnp.max(s_t, axis=0, keepdims=True)


def _flash_scores(q_ref, q_cols, k_ref, s_ref, mx_ref, sup, step, plan, slot):
    for a, _, n_keys in plan:
        for c in range(_n_softmax(q_cols)):
            _score_chain(q_ref, q_cols, k_ref, s_ref, mx_ref, sup, step, a, c, slot, n_keys)


EVERYONE = [(a, None, K_STEP) for a in range(Q_SUB)]


def _flash_super_tile(q_ref, q_cols, k_ref, vt_ref, s_ref, mx_ref, m_ref, acc_ref, sup, prefetch_next, finalize):
    n = _n_softmax(q_cols)
    m_ref[...] = jnp.full(m_ref.shape, NEG_BIG, F32)
    acc_ref[...] = jnp.zeros(acc_ref.shape, F32)
    first_step = sup * STEPS_PER_SUPER
    ones = jnp.ones((ONES_ROWS, K_STEP), BF16)

    def scores(step, plan, slot):
        _flash_scores(q_ref, q_cols, k_ref, s_ref, mx_ref, sup, step, plan, slot)

    def update(step, plan, slot, after_chain=None, before_chain=None):
        vt_aug = jnp.concatenate([vt_ref[0, 0, step], ones], axis=0)
        for a, offset, n_keys in plan:
            for c in range(n):
                ch = a * n + c
                if before_chain is not None:
                    before_chain(a, c)
                if offset is None:
                    s_t, mx = s_ref[slot, ch], mx_ref[slot, ch]
                else:
                    s_t = jnp.where(_causal_keep(offset, n_keys), s_ref[slot, ch, :n_keys], NEG_BIG)
                    mx = jnp.max(s_t, axis=0, keepdims=True)
                m_old = m_ref[ch]
                m_new = jnp.maximum(m_old, mx)
                p = jnp.exp2(s_t - m_new).astype(BF16)
                acc_ref[ch] = jnp.exp2(m_old - m_new) * acc_ref[ch] + _dot(vt_aug[:, :n_keys], p)
                m_ref[ch] = m_new
                if after_chain is not None:
                    after_chain(a, c)

    def pair_body(jj, carry):
        step = 2 * jj
        update(step, EVERYONE, 0, before_chain=functools.partial(
            _score_chain, q_ref, q_cols, k_ref, s_ref, mx_ref, sup, step + 1, slot=1))
        update(step + 1, EVERYONE, 1, before_chain=functools.partial(
            _score_chain, q_ref, q_cols, k_ref, s_ref, mx_ref, sup, step + 2, slot=0))
        return carry

    lax.fori_loop(0, sup * (STEPS_PER_SUPER // 2), pair_body, 0)
    for e in range(STEPS_PER_SUPER):
        plan = _diag_plan(e)[::-1]
        ahead = None
        if e + 1 < STEPS_PER_SUPER:
            keys_next = {a: n_keys for a, _, n_keys in _diag_plan(e + 1)}

            def ahead(a, c, e=e, keys_next=keys_next):
                if a in keys_next:
                    _score_chain(q_ref, q_cols, k_ref, s_ref, mx_ref, sup, first_step + e + 1, a, c, (e + 1) % 2,
                                 keys_next[a])
        refill = None
        if prefetch_next and e == STEPS_PER_SUPER - 2:
            assert e % 2 == 0
            refill = functools.partial(_score_chain, q_ref, q_cols, k_ref, s_ref, mx_ref, sup + 1, 0, slot=0)
            for a in range(Q_SUB):
                if a not in [b for b, _, _ in plan]:
                    for c in range(n):
                        refill(a, c)
        update(first_step + e, plan, e % 2, refill, ahead)
        for a in range(Q_SUB):
            if (Q_TILE * a + Q_TILE - 1) // K_STEP == e:
                finalize(a)


def _run_super_tiles(super_tile, q_ref, q_cols, k_ref, s_ref, mx_ref, n_super):
    _flash_scores(q_ref, q_cols, k_ref, s_ref, mx_ref, 0, 0, _diag_plan(0), 0)

    def body(sup, carry):
        super_tile(sup, True)
        return carry

    lax.fori_loop(0, n_super - 1, body, 0)
    super_tile(n_super - 1, False)


def _softmax_out(acc_ref, ch):
    acc = acc_ref[ch]
    return acc[:V_DIM] * (1.0 / acc[V_DIM:V_DIM + 1])


def _diff_attn_kernel(qz_ref, k_ref, vt_ref, g_ref, lq1_ref, lk1_ref, lq2_ref, lk2_ref, o_ref,
                      s_ref, mx_ref, m_ref, acc_ref, *, lambda_init):
    lam = (jnp.exp(jnp.sum(lq1_ref[...] * lk1_ref[...], keepdims=True))
           - jnp.exp(jnp.sum(lq2_ref[...] * lk2_ref[...], keepdims=True)) + lambda_init)
    g_col = g_ref[...]
    q_cols = [slice(0, LANES), slice(LANES, SLOT)]

    def super_tile(sup, prefetch_next):
        def finalize(a):
            o_t = _softmax_out(acc_ref, 2 * a) - lam * _softmax_out(acc_ref, 2 * a + 1)
            y_t = o_t * lax.rsqrt(jnp.mean(o_t * o_t, axis=0, keepdims=True) + SUBLN_EPS) * g_col
            rows = pl.ds(pl.multiple_of((sup * Q_SUB + a) * Q_TILE, Q_TILE), Q_TILE)
            o_ref[0, rows, :] = (y_t * (1.0 - lambda_init)).T.astype(BF16)

        _flash_super_tile(qz_ref, q_cols, k_ref, vt_ref, s_ref, mx_ref, m_ref, acc_ref, sup, prefetch_next, finalize)

    _run_super_tiles(super_tile, qz_ref, q_cols, k_ref, s_ref, mx_ref, o_ref.shape[1] // (Q_TILE * Q_SUB))


def _mla_attn_kernel(q_ref, k_ref, vt_ref, o_ref, s_ref, mx_ref, m_ref, acc_ref):
    q_cols = None

    def super_tile(sup, prefetch_next):
        def finalize(a):
            rows = pl.ds(pl.multiple_of((sup * Q_SUB + a) * Q_TILE, Q_TILE), Q_TILE)
            o_ref[0, rows, :] = _softmax_out(acc_ref, a).T.astype(BF16)

        _flash_super_tile(q_ref, q_cols, k_ref, vt_ref, s_ref, mx_ref, m_ref, acc_ref, sup, prefetch_next, finalize)

    _run_super_tiles(super_tile, q_ref, q_cols, k_ref, s_ref, mx_ref, o_ref.shape[1] // (Q_TILE * Q_SUB))


def _attn_call(kernel, n_softmax, q, k, k_width, vt, extra, name):
    b, s, _ = k.shape
    chains = n_softmax * Q_SUB
    small = lambda bi, hi: (0, 0)
    if q.ndim == 3:
        q_spec = pl.BlockSpec((1, s, SLOT), lambda bi, hi: (bi, 0, hi))
    else:
        q_spec = pl.BlockSpec((1, 1) + q.shape[2:], lambda bi, hi: (bi, hi, 0, 0, 0))
    in_specs = [
        q_spec,
        pl.BlockSpec((1, s, k_width), lambda bi, hi: (bi, 0, hi)),
        pl.BlockSpec((1, 1, s // K_STEP, V_DIM, K_STEP), lambda bi, hi: (bi, hi, 0, 0, 0)),
    ] + [pl.BlockSpec(e.shape, small) for e in extra]
    return pl.pallas_call(
        kernel,
        grid=(b, HEADS),
        in_specs=in_specs,
        out_specs=pl.BlockSpec((1, s, V_DIM), lambda bi, hi: (bi, 0, hi)),
        out_shape=jax.ShapeDtypeStruct((b, s, HEADS * V_DIM), BF16),
        scratch_shapes=[
            pltpu.VMEM((2, chains, K_STEP, Q_TILE), F32),
            pltpu.VMEM((2, chains, 1, Q_TILE), F32),
            pltpu.VMEM((chains, 1, Q_TILE), F32),
            pltpu.VMEM((chains, V_DIM + ONES_ROWS, Q_TILE), F32),
        ],
        compiler_params=pltpu.CompilerParams(dimension_semantics=("arbitrary", "arbitrary"),
                                             vmem_limit_bytes=VMEM_LIMIT),
        name=name,
    )(q, k, vt, *extra)


def _post_kernel(h_ref, o_ref, wo_ref, g_ref, wup_ref, wdn_ref, gf_ref, out_ref, *, final_norm):
    h1 = h_ref[...] + _dot(o_ref[...], wo_ref[...])
    hn = _rms(h1, g_ref[...], EPS).astype(BF16)
    acc = h1
    for f in range(D_FF // FF_CHUNK):
        cols = slice(f * FF_CHUNK, (f + 1) * FF_CHUNK)
        u = jnp.maximum(_dot(hn, wup_ref[:, cols]), 0.0)
        acc = acc + _dot((u * u).astype(BF16), wdn_ref[cols, :])
    if final_norm:
        acc = _rms(acc, gf_ref[...], EPS)
    out_ref[...] = acc


def _post(h2, o2, w_o, g_mlp, w_up, w_down, g_final, final_norm, name):
    t = h2.shape[0]
    tm = MLP_TILE
    const = lambda i: (0, 0)
    row = lambda i: (i, 0)
    resident = functools.partial(pl.BlockSpec, index_map=const, pipeline_mode=pl.Buffered(1))
    return pl.pallas_call(
        functools.partial(_post_kernel, final_norm=final_norm),
        grid=(t // tm,),
        in_specs=[
            pl.BlockSpec((tm, D_MODEL), row),
            pl.BlockSpec((tm, D_MODEL), row),
            resident((D_MODEL, D_MODEL)),
            pl.BlockSpec((1, D_MODEL), const),
            resident((D_MODEL, D_FF)),
            resident((D_FF, D_MODEL)),
            pl.BlockSpec((1, D_MODEL), const),
        ],
        out_specs=pl.BlockSpec((tm, D_MODEL), row),
        out_shape=jax.ShapeDtypeStruct((t, D_MODEL), F32),
        compiler_params=pltpu.CompilerParams(dimension_semantics=("arbitrary",), vmem_limit_bytes=VMEM_LIMIT),
        name=name,
    )(h2, o2, w_o, g_mlp, w_up, w_down, g_final)


def _proj_b_kernel(h_ref, gkv_ref, wdkv_ref, gc_ref, wuk_ref, wuv_ref, gq_ref, wdq_ref, gcq_ref, wuqt_ref,
                   cos_ref, sin_ref, cost_ref, sint_ref, q_ref, k_ref, vt_ref):
    h = h_ref[...]
    cos, sin = cos_ref[...], sin_ref[...]
    ckv = _dot(_rms(h, gkv_ref[...], EPS).astype(BF16), wdkv_ref[...])
    cq = _dot(_rms(h, gq_ref[...], EPS).astype(BF16), wdq_ref[...])
    c = _rms(ckv[:, :KV_LORA], gc_ref[...], EPS).astype(BF16)
    k_rope = _rope_slab(ckv[:, KV_LORA:], cos, sin).astype(BF16)
    k_nope = _dot(c, wuk_ref[...]).astype(BF16)
    for hd in range(HEADS):
        k_ref[:, hd * SLOT:hd * SLOT + LANES] = k_nope[:, hd * LANES:(hd + 1) * LANES]
        k_ref[:, hd * SLOT + LANES:(hd + 1) * SLOT] = k_rope
    v = _dot(c, wuv_ref[...])
    for hd in range(HEADS):
        vt_ref[0, hd, 0] = v[:, hd * V_DIM:(hd + 1) * V_DIM].T.astype(BF16)
    q_t = _dot(wuqt_ref[...], _rms(cq, gcq_ref[...], EPS).T.astype(BF16))
    cos_t, sin_t = cost_ref[...], sint_ref[...]
    scale = (MLA_NOPE + MLA_ROPE) ** -0.5 * LOG2_E
    half = MLA_ROPE // 2
    pad = jnp.zeros((SLOT - MLA_NOPE - MLA_ROPE, q_t.shape[1]), F32)
    for hd in range(HEADS):
        base = hd * (MLA_NOPE + MLA_ROPE)
        x1 = q_t[base + MLA_NOPE:base + MLA_NOPE + half]
        x2 = q_t[base + MLA_NOPE + half:base + MLA_NOPE + MLA_ROPE]
        tile = jnp.concatenate([q_t[base:base + MLA_NOPE], x1 * cos_t - x2 * sin_t, x2 * cos_t + x1 * sin_t, pad],
                               axis=0) * scale
        for j in range(q_t.shape[1] // Q_TILE):
            q_ref[0, hd, j] = tile[:, j * Q_TILE:(j + 1) * Q_TILE].astype(BF16)


def _proj_b(h2, g_kv_in, w_dkv, g_c, w_uk, w_uv, g_q, w_dq, g_cq, w_uq_t, cos, sin, cos_t, sin_t, seq):
    t = h2.shape[0]
    tm = TOKEN_TILE
    pos_blocks = seq // tm
    const = lambda i: (0, 0)
    row = lambda i: (i, 0)
    full = lambda a: pl.BlockSpec(a.shape, const)
    return pl.pallas_call(
        _proj_b_kernel,
        grid=(t // tm,),
        in_specs=[
            pl.BlockSpec((tm, D_MODEL), row),
            full(g_kv_in), full(w_dkv), full(g_c), full(w_uk), full(w_uv),
            full(g_q), full(w_dq), full(g_cq), full(w_uq_t),
            pl.BlockSpec((tm, LANES), lambda i: (i % pos_blocks, 0)),
            pl.BlockSpec((tm, LANES), lambda i: (i % pos_blocks, 0)),
            pl.BlockSpec((MLA_ROPE // 2, tm), lambda i: (0, i % pos_blocks)),
            pl.BlockSpec((MLA_ROPE // 2, tm), lambda i: (0, i % pos_blocks)),
        ],
        out_specs=[
            pl.BlockSpec((1, HEADS, tm // Q_TILE, SLOT, Q_TILE), lambda i: (i // pos_blocks, 0, i % pos_blocks, 0, 0)),
            pl.BlockSpec((tm, HEADS * SLOT), row),
            _values_t_spec(pos_blocks),
        ],
        out_shape=[
            jax.ShapeDtypeStruct((t // seq, HEADS, seq // Q_TILE, SLOT, Q_TILE), BF16),
            jax.ShapeDtypeStruct((t, HEADS * SLOT), BF16),
            _values_t_shape(t, seq),
        ],
        compiler_params=pltpu.CompilerParams(dimension_semantics=("arbitrary",), vmem_limit_bytes=VMEM_LIMIT),
        name="proj_b",
    )(h2, g_kv_in, w_dkv, g_c, w_uk, w_uv, g_q, w_dq, g_cq, w_uq_t, cos, sin, cos_t, sin_t)


def _rope_tables(seq):
    pos = jnp.arange(seq, dtype=F32)
    inv_freq = ROPE_THETA ** (-jnp.arange(0, HEAD_DIM, 2, dtype=F32) / HEAD_DIM)
    ang = pos[:, None] * inv_freq[None, :]
    cos, sin = jnp.cos(ang), jnp.sin(ang)
    return jnp.tile(cos, (1, 4)), jnp.tile(jnp.concatenate([-sin, sin], axis=1), (1, 2)), cos.T, sin.T


def _pad_cols(w, width):
    return jnp.pad(w, ((0, 0), (0, width - w.shape[1])))


def kernel(x, attn_norm_g, w_qkv_a, lambda_q1, lambda_k1, lambda_q2, lambda_k2, subln_g, w_o_a, kv_in_norm_g, w_dkv, kv_norm_g, w_ukv, w_dq, q_norm_g, w_uq, w_o_b, mlp_norm_g, w_up, w_down, final_norm_g):
    b, s, _ = x.shape
    t = b * s
    cos, sin, cos_t, sin_t = _rope_tables(s)
    row = lambda g: g.reshape(1, -1)
    h = x.reshape(t, D_MODEL)

    lambda_init = 0.8 - 0.6 * math.exp(-0.3 * 0)
    qz, k, vt = _proj_a(h, row(attn_norm_g[0]), w_qkv_a[0].astype(BF16), cos, sin, s)
    lam_rows = [row(p[0]) for p in (lambda_q1, lambda_k1, lambda_q2, lambda_k2)]
    o = _attn_call(functools.partial(_diff_attn_kernel, lambda_init=lambda_init), 2,
                   qz.reshape(b, s, HEADS * SLOT), k.reshape(b, s, D_MODEL), LANES,
                   vt, [subln_g[0].reshape(V_DIM, 1)] + lam_rows, "diff_attn")
    h = _post(h, o.reshape(t, D_MODEL), w_o_a[0].astype(BF16), row(mlp_norm_g[0]),
              w_up[0].astype(BF16), w_down[0].astype(BF16), row(final_norm_g), False, "post_a")

    w_ukv_h = w_ukv.reshape(KV_LORA, HEADS, MLA_NOPE + V_DIM)
    w_uk = w_ukv_h[:, :, :MLA_NOPE].reshape(KV_LORA, HEADS * MLA_NOPE)
    w_uv = w_ukv_h[:, :, MLA_NOPE:].reshape(KV_LORA, HEADS * V_DIM)
    q, kk, vt = _proj_b(h, row(kv_in_norm_g), _pad_cols(w_dkv, KV_LORA + LANES).astype(BF16), row(kv_norm_g),
                       w_uk.astype(BF16), w_uv.astype(BF16), row(attn_norm_g[1]), w_dq[0].astype(BF16),
                       row(q_norm_g[0]), w_uq[0].T.astype(BF16), cos, sin, cos_t, sin_t, s)
    o = _attn_call(_mla_attn_kernel, 1, q, kk.reshape(b, s, HEADS * SLOT), SLOT,
                   vt, [], "mla_attn")
    h = _post(h, o.reshape(t, D_MODEL), w_o_b[0].astype(BF16), row(mlp_norm_g[1]),
              w_up[1].astype(BF16), w_down[1].astype(BF16), row(final_norm_g), True, "post_b")
    return h.reshape(b, s, D_MODEL)
```

```python
import functools
import math

import jax
import jax.numpy as jnp
from jax import lax
from jax.experimental import pallas as pl
from jax.experimental.pallas import tpu as pltpu

D_MODEL = 1024
DEPTH = 2
N_A = DEPTH // 2
HEADS = 8
HEAD_DIM = 64
V_DIM = 128
MLA_NOPE = 128
MLA_ROPE = 64
Q_LORA = 384
KV_LORA = 256
D_FF = 4 * D_MODEL
ROPE_THETA = 10000.0
EPS = 1e-6
SUBLN_EPS = 1e-5

LANES = 128
SLOT = 2 * LANES
TOKEN_TILE = 1024
MLP_TILE = 1024
Q_TILE = 256
K_STEP = 512
Q_SUB = 8
STEPS_PER_SUPER = Q_SUB * Q_TILE // K_STEP
ONES_ROWS = 16
FF_CHUNK = 1024
NEG_BIG = -1e30
LOG2_E = math.log2(math.e)
VMEM_LIMIT = 56 * 1024 * 1024

BF16 = jnp.bfloat16
F32 = jnp.float32
NT_DIMS = (((1,), (1,)), ((), ()))


def _rms(x, g, eps):
    return x * lax.rsqrt(jnp.mean(x * x, axis=-1, keepdims=True) + eps) * g


def _rope_slab(x, cos, sin_signed):
    lane = lax.broadcasted_iota(jnp.int32, x.shape, 1)
    first = (lane % HEAD_DIM) < (HEAD_DIM // 2)
    swapped = jnp.where(first, pltpu.roll(x, LANES - HEAD_DIM // 2, 1), pltpu.roll(x, HEAD_DIM // 2, 1))
    return x * cos + swapped * sin_signed


def _dot(a, b):
    return jnp.dot(a, b, preferred_element_type=F32)


def _values_t_spec(steps_per_seq):
    return pl.BlockSpec((1, HEADS, TOKEN_TILE // K_STEP, V_DIM, K_STEP),
                        lambda i: (i // steps_per_seq, 0, i % steps_per_seq, 0, 0))


def _values_t_shape(tokens, seq):
    return jax.ShapeDtypeStruct((tokens // seq, HEADS, seq // K_STEP, V_DIM, K_STEP), BF16)


def _proj_a_kernel(x_ref, g_ref, w_ref, cos_ref, sin_ref, qz_ref, k_ref, vt_ref):
    hn = _rms(x_ref[...], g_ref[...], EPS).astype(BF16)
    qkv = _dot(hn, w_ref[...])
    cos, sin = cos_ref[...], sin_ref[...]
    lane = lax.broadcasted_iota(jnp.int32, (x_ref.shape[0], LANES), 1)
    comp1 = lane < HEAD_DIM
    scale = HEAD_DIM ** -0.5 * LOG2_E
    for h in range(HEADS):
        q = _rope_slab(qkv[:, h * LANES:(h + 1) * LANES], cos, sin) * scale
        qz_ref[:, h * SLOT:h * SLOT + LANES] = jnp.where(comp1, q, 0.0).astype(BF16)
        qz_ref[:, h * SLOT + LANES:(h + 1) * SLOT] = jnp.where(comp1, 0.0, q).astype(BF16)
        k = _rope_slab(qkv[:, D_MODEL + h * LANES:D_MODEL + (h + 1) * LANES], cos, sin)
        k_ref[:, h * LANES:(h + 1) * LANES] = k.astype(BF16)
        for j in range(TOKEN_TILE // K_STEP):
            v = qkv[j * K_STEP:(j + 1) * K_STEP, 2 * D_MODEL + h * V_DIM:2 * D_MODEL + (h + 1) * V_DIM]
            vt_ref[0, h, j] = v.T.astype(BF16)


def _proj_a(x2, g, w_qkv, cos, sin, seq):
    t = x2.shape[0]
    tm = TOKEN_TILE
    pos_blocks = seq // tm
    const = lambda i: (0, 0)
    return pl.pallas_call(
        _proj_a_kernel,
        grid=(t // tm,),
        in_specs=[
            pl.BlockSpec((tm, D_MODEL), lambda i: (i, 0)),
            pl.BlockSpec((1, D_MODEL), const),
            pl.BlockSpec((D_MODEL, 3 * D_MODEL), const),
            pl.BlockSpec((tm, LANES), lambda i: (i % pos_blocks, 0)),
            pl.BlockSpec((tm, LANES), lambda i: (i % pos_blocks, 0)),
        ],
        out_specs=[
            pl.BlockSpec((tm, HEADS * SLOT), lambda i: (i, 0)),
            pl.BlockSpec((tm, D_MODEL), lambda i: (i, 0)),
            _values_t_spec(pos_blocks),
        ],
        out_shape=[
            jax.ShapeDtypeStruct((t, HEADS * SLOT), BF16),
            jax.ShapeDtypeStruct((t, D_MODEL), BF16),
            _values_t_shape(t, seq),
        ],
        compiler_params=pltpu.CompilerParams(dimension_semantics=("arbitrary",), vmem_limit_bytes=VMEM_LIMIT),
        name="proj_a",
    )(x2, g, w_qkv, cos, sin)


def _causal_keep(offset, n_keys):
    key = lax.broadcasted_iota(jnp.int32, (n_keys, Q_TILE), 0)
    qry = lax.broadcasted_iota(jnp.int32, (n_keys, Q_TILE), 1)
    return key <= qry + offset


def _diag_plan(e):
    plan = []
    for a in range(Q_SUB):
        offset = Q_TILE * a - K_STEP * e
        if offset + Q_TILE > 0:
            plan.append((a, None if offset >= K_STEP - 1 else offset, min(K_STEP, offset + Q_TILE)))
    return plan


def _n_softmax(q_cols):
    return 1 if q_cols is None else len(q_cols)


def _score_chain(q_ref, q_cols, k_ref, s_ref, mx_ref, sup, step, a, c, slot, n_keys=K_STEP):
    k_t = k_ref[0, pl.ds(pl.multiple_of(step * K_STEP, K_STEP), n_keys), :]
    if q_cols is None:
        s_t = _dot(k_t, q_ref[0, 0, sup * Q_SUB + a])
    else:
        rows = pl.ds(pl.multiple_of((sup * Q_SUB + a) * Q_TILE, Q_TILE), Q_TILE)
        s_t = lax.dot_general(k_t, q_ref[0, rows, q_cols[c]], NT_DIMS, preferred_element_type=F32)
    ch = a * _n_softmax(q_cols) + c
    s_ref[slot, ch, :n_keys] = s_t
    mx_ref[slot, ch] = jnp.max(s_t, axis=0, keepdims=True)


def _flash_scores(q_ref, q_cols, k_ref, s_ref, mx_ref, sup, step, plan, slot):
    for a, _, n_keys in plan:
        for c in range(_n_softmax(q_cols)):
            _score_chain(q_ref, q_cols, k_ref, s_ref, mx_ref, sup, step, a, c, slot, n_keys)


EVERYONE = [(a, None, K_STEP) for a in range(Q_SUB)]


def _flash_super_tile(q_ref, q_cols, k_ref, vt_ref, s_ref, mx_ref, m_ref, acc_ref, sup, prefetch_next, finalize):
    n = _n_softmax(q_cols)
    m_ref[...] = jnp.full(m_ref.shape, NEG_BIG, F32)
    acc_ref[...] = jnp.zeros(acc_ref.shape, F32)
    first_step = sup * STEPS_PER_SUPER
    ones = jnp.ones((ONES_ROWS, K_STEP), BF16)

    def scores(step, plan, slot):
        _flash_scores(q_ref, q_cols, k_ref, s_ref, mx_ref, sup, step, plan, slot)

    def update(step, plan, slot, after_chain=None, before_chain=None):
        vt_aug = jnp.concatenate([vt_ref[0, 0, step], ones], axis=0)
        for a, offset, n_keys in plan:
            for c in range(n):
                ch = a * n + c
                if before_chain is not None:
                    before_chain(a, c)
                if offset is None:
                    s_t, mx = s_ref[slot, ch], mx_ref[slot, ch]
                else:
                    s_t = jnp.where(_causal_keep(offset, n_keys), s_ref[slot, ch, :n_keys], NEG_BIG)
                    mx = jnp.max(s_t, axis=0, keepdims=True)
                m_old = m_ref[ch]
                m_new = jnp.maximum(m_old, mx)
                p = jnp.exp2(s_t - m_new).astype(BF16)
                acc_ref[ch] = jnp.exp2(m_old - m_new) * acc_ref[ch] + _dot(vt_aug[:, :n_keys], p)
                m_ref[ch] = m_new
                if after_chain is not None:
                    after_chain(a, c)

    def pair_body(jj, carry):
        step = 2 * jj
        update(step, EVERYONE, 0, before_chain=functools.partial(
            _score_chain, q_ref, q_cols, k_ref, s_ref, mx_ref, sup, step + 1, slot=1))
        update(step + 1, EVERYONE, 1, before_chain=functools.partial(
            _score_chain, q_ref, q_cols, k_ref, s_ref, mx_ref, sup, step + 2, slot=0))
        return carry

    lax.fori_loop(0, sup * (STEPS_PER_SUPER // 2), pair_body, 0)
    for e in range(STEPS_PER_SUPER):
        plan = _diag_plan(e)[::-1]
        ahead = None
        if e + 1 < STEPS_PER_SUPER:
            keys_next = {a: n_keys for a, _, n_keys in _diag_plan(e + 1)}

            def ahead(a, c, e=e, keys_next=keys_next):
                if a in keys_next:
                    _score_chain(q_ref, q_cols, k_ref, s_ref, mx_ref, sup, first_step + e + 1, a, c, (e + 1) % 2,
                                 keys_next[a])
        refill = None
        if prefetch_next and e == STEPS_PER_SUPER - 2:
            assert e % 2 == 0
            refill = functools.partial(_score_chain, q_ref, q_cols, k_ref, s_ref, mx_ref, sup + 1, 0, slot=0)
            for a in range(Q_SUB):
                if a not in [b for b, _, _ in plan]:
                    for c in range(n):
                        refill(a, c)
        update(first_step + e, plan, e % 2, refill, ahead)
        for a in range(Q_SUB):
            if (Q_TILE * a + Q_TILE - 1) // K_STEP == e:
                finalize(a)


def _run_super_tiles(super_tile, q_ref, q_cols, k_ref, s_ref, mx_ref, n_super):
    _flash_scores(q_ref, q_cols, k_ref, s_ref, mx_ref, 0, 0, _diag_plan(0), 0)

    def body(sup, carry):
        super_tile(sup, True)
        return carry

    lax.fori_loop(0, n_super - 1, body, 0)
    super_tile(n_super - 1, False)


def _softmax_out(acc_ref, ch):
    acc = acc_ref[ch]
    return acc[:V_DIM] * (1.0 / acc[V_DIM:V_DIM + 1])


def _diff_attn_kernel(qz_ref, k_ref, vt_ref, g_ref, lq1_ref, lk1_ref, lq2_ref, lk2_ref, o_ref,
                      s_ref, mx_ref, m_ref, acc_ref, *, lambda_init):
    lam = (jnp.exp(jnp.sum(lq1_ref[...] * lk1_ref[...], keepdims=True))
           - jnp.exp(jnp.sum(lq2_ref[...] * lk2_ref[...], keepdims=True)) + lambda_init)
    g_col = g_ref[...]
    q_cols = [slice(0, LANES), slice(LANES, SLOT)]

    def super_tile(sup, prefetch_next):
        def finalize(a):
            o_t = _softmax_out(acc_ref, 2 * a) - lam * _softmax_out(acc_ref, 2 * a + 1)
            y_t = o_t * lax.rsqrt(jnp.mean(o_t * o_t, axis=0, keepdims=True) + SUBLN_EPS) * g_col
            rows = pl.ds(pl.multiple_of((sup * Q_SUB + a) * Q_TILE, Q_TILE), Q_TILE)
            o_ref[0, rows, :] = (y_t * (1.0 - lambda_init)).T.astype(BF16)

        _flash_super_tile(qz_ref, q_cols, k_ref, vt_ref, s_ref, mx_ref, m_ref, acc_ref, sup, prefetch_next, finalize)

    _run_super_tiles(super_tile, qz_ref, q_cols, k_ref, s_ref, mx_ref, o_ref.shape[1] // (Q_TILE * Q_SUB))


def _mla_attn_kernel(q_ref, k_ref, vt_ref, o_ref, s_ref, mx_ref, m_ref, acc_ref):
    q_cols = None

    def super_tile(sup, prefetch_next):
        def finalize(a):
            rows = pl.ds(pl.multiple_of((sup * Q_SUB + a) * Q_TILE, Q_TILE), Q_TILE)
            o_ref[0, rows, :] = _softmax_out(acc_ref, a).T.astype(BF16)

        _flash_super_tile(q_ref, q_cols, k_ref, vt_ref, s_ref, mx_ref, m_ref, acc_ref, sup, prefetch_next, finalize)

    _run_super_tiles(super_tile, q_ref, q_cols, k_ref, s_ref, mx_ref, o_ref.shape[1] // (Q_TILE * Q_SUB))


def _attn_call(kernel, n_softmax, q, k, k_width, vt, extra, name):
    b, s, _ = k.shape
    chains = n_softmax * Q_SUB
    small = lambda bi, hi: (0, 0)
    if q.ndim == 3:
        q_spec = pl.BlockSpec((1, s, SLOT), lambda bi, hi: (bi, 0, hi))
    else:
        q_spec = pl.BlockSpec((1, 1) + q.shape[2:], lambda bi, hi: (bi, hi, 0, 0, 0))
    in_specs = [
        q_spec,
        pl.BlockSpec((1, s, k_width), lambda bi, hi: (bi, 0, hi)),
        pl.BlockSpec((1, 1, s // K_STEP, V_DIM, K_STEP), lambda bi, hi: (bi, hi, 0, 0, 0)),
    ] + [pl.BlockSpec(e.shape, small) for e in extra]
    return pl.pallas_call(
        kernel,
        grid=(b, HEADS),
        in_specs=in_specs,
        out_specs=pl.BlockSpec((1, s, V_DIM), lambda bi, hi: (bi, 0, hi)),
        out_shape=jax.ShapeDtypeStruct((b, s, HEADS * V_DIM), BF16),
        scratch_shapes=[
            pltpu.VMEM((2, chains, K_STEP, Q_TILE), F32),
            pltpu.VMEM((2, chains, 1, Q_TILE), F32),
            pltpu.VMEM((chains, 1, Q_TILE), F32),
            pltpu.VMEM((chains, V_DIM + ONES_ROWS, Q_TILE), F32),
        ],
        compiler_params=pltpu.CompilerParams(dimension_semantics=("arbitrary", "arbitrary"),
                                             vmem_limit_bytes=VMEM_LIMIT),
        name=name,
    )(q, k, vt, *extra)


def _post_kernel(h_ref, o_ref, wo_ref, g_ref, wup_ref, wdn_ref, gf_ref, out_ref, *, final_norm):
    h1 = h_ref[...] + _dot(o_ref[...], wo_ref[...])
    hn = _rms(h1, g_ref[...], EPS).astype(BF16)
    acc = h1
    for f in range(D_FF // FF_CHUNK):
        cols = slice(f * FF_CHUNK, (f + 1) * FF_CHUNK)
        u = jnp.maximum(_dot(hn, wup_ref[:, cols]), 0.0)
        acc = acc + _dot((u * u).astype(BF16), wdn_ref[cols, :])
    if final_norm:
        acc = _rms(acc, gf_ref[...], EPS)
    out_ref[...] = acc


def _post(h2, o2, w_o, g_mlp, w_up, w_down, g_final, final_norm, name):
    t = h2.shape[0]
    tm = MLP_TILE
    const = lambda i: (0, 0)
    row = lambda i: (i, 0)
    resident = functools.partial(pl.BlockSpec, index_map=const, pipeline_mode=pl.Buffered(1))
    return pl.pallas_call(
        functools.partial(_post_kernel, final_norm=final_norm),
        grid=(t // tm,),
        in_specs=[
            pl.BlockSpec((tm, D_MODEL), row),
            pl.BlockSpec((tm, D_MODEL), row),
            resident((D_MODEL, D_MODEL)),
            pl.BlockSpec((1, D_MODEL), const),
            resident((D_MODEL, D_FF)),
            resident((D_FF, D_MODEL)),
            pl.BlockSpec((1, D_MODEL), const),
        ],
        out_specs=pl.BlockSpec((tm, D_MODEL), row),
        out_shape=jax.ShapeDtypeStruct((t, D_MODEL), F32),
        compiler_params=pltpu.CompilerParams(dimension_semantics=("arbitrary",), vmem_limit_bytes=VMEM_LIMIT),
        name=name,
    )(h2, o2, w_o, g_mlp, w_up, w_down, g_final)


def _proj_b_kernel(h_ref, gkv_ref, wdkv_ref, gc_ref, wuk_ref, wuv_ref, gq_ref, wdq_ref, gcq_ref, wuqt_ref,
                   cos_ref, sin_ref, cost_ref, sint_ref, q_ref, k_ref, vt_ref):
    h = h_ref[...]
    cos, sin = cos_ref[...], sin_ref[...]
    ckv = _dot(_rms(h, gkv_ref[...], EPS).astype(BF16), wdkv_ref[...])
    cq = _dot(_rms(h, gq_ref[...], EPS).astype(BF16), wdq_ref[...])
    c = _rms(ckv[:, :KV_LORA], gc_ref[...], EPS).astype(BF16)
    k_rope = _rope_slab(ckv[:, KV_LORA:], cos, sin).astype(BF16)
    k_nope = _dot(c, wuk_ref[...]).astype(BF16)
    for hd in range(HEADS):
        k_ref[:, hd * SLOT:hd * SLOT + LANES] = k_nope[:, hd * LANES:(hd + 1) * LANES]
        k_ref[:, hd * SLOT + LANES:(hd + 1) * SLOT] = k_rope
    v = _dot(c, wuv_ref[...])
    for hd in range(HEADS):
        for j in range(TOKEN_TILE // K_STEP):
            vt_ref[0, hd, j] = v[j * K_STEP:(j + 1) * K_STEP, hd * V_DIM:(hd + 1) * V_DIM].T.astype(BF16)
    q_t = _dot(wuqt_ref[...], _rms(cq, gcq_ref[...], EPS).T.astype(BF16))
    cos_t, sin_t = cost_ref[...], sint_ref[...]
    scale = (MLA_NOPE + MLA_ROPE) ** -0.5 * LOG2_E
    half = MLA_ROPE // 2
    pad = jnp.zeros((SLOT - MLA_NOPE - MLA_ROPE, q_t.shape[1]), F32)
    for hd in range(HEADS):
        base = hd * (MLA_NOPE + MLA_ROPE)
        x1 = q_t[base + MLA_NOPE:base + MLA_NOPE + half]
        x2 = q_t[base + MLA_NOPE + half:base + MLA_NOPE + MLA_ROPE]
        tile = jnp.concatenate([q_t[base:base + MLA_NOPE], x1 * cos_t - x2 * sin_t, x2 * cos_t + x1 * sin_t, pad],
                               axis=0) * scale
        for j in range(q_t.shape[1] // Q_TILE):
            q_ref[0, hd, j] = tile[:, j * Q_TILE:(j + 1) * Q_TILE].astype(BF16)


def _proj_b(h2, g_kv_in, w_dkv, g_c, w_uk, w_uv, g_q, w_dq, g_cq, w_uq_t, cos, sin, cos_t, sin_t, seq):
    t = h2.shape[0]
    tm = TOKEN_TILE
    pos_blocks = seq // tm
    const = lambda i: (0, 0)
    row = lambda i: (i, 0)
    full = lambda a: pl.BlockSpec(a.shape, const)
    return pl.pallas_call(
        _proj_b_kernel,
        grid=(t // tm,),
        in_specs=[
            pl.BlockSpec((tm, D_MODEL), row),
            full(g_kv_in), full(w_dkv), full(g_c), full(w_uk), full(w_uv),
            full(g_q), full(w_dq), full(g_cq), full(w_uq_t),
            pl.BlockSpec((tm, LANES), lambda i: (i % pos_blocks, 0)),
            pl.BlockSpec((tm, LANES), lambda i: (i % pos_blocks, 0)),
            pl.BlockSpec((MLA_ROPE // 2, tm), lambda i: (0, i % pos_blocks)),
            pl.BlockSpec((MLA_ROPE // 2, tm), lambda i: (0, i % pos_blocks)),
        ],
        out_specs=[
            pl.BlockSpec((1, HEADS, tm // Q_TILE, SLOT, Q_TILE), lambda i: (i // pos_blocks, 0, i % pos_blocks, 0, 0)),
            pl.BlockSpec((tm, HEADS * SLOT), row),
            _values_t_spec(pos_blocks),
        ],
        out_shape=[
            jax.ShapeDtypeStruct((t // seq, HEADS, seq // Q_TILE, SLOT, Q_TILE), BF16),
            jax.ShapeDtypeStruct((t, HEADS * SLOT), BF16),
            _values_t_shape(t, seq),
        ],
        compiler_params=pltpu.CompilerParams(dimension_semantics=("arbitrary",), vmem_limit_bytes=VMEM_LIMIT),
        name="proj_b",
    )(h2, g_kv_in, w_dkv, g_c, w_uk, w_uv, g_q, w_dq, g_cq, w_uq_t, cos, sin, cos_t, sin_t)


def _rope_tables(seq):
    pos = jnp.arange(seq, dtype=F32)
    inv_freq = ROPE_THETA ** (-jnp.arange(0, HEAD_DIM, 2, dtype=F32) / HEAD_DIM)
    ang = pos[:, None] * inv_freq[None, :]
    cos, sin = jnp.cos(ang), jnp.sin(ang)
    return jnp.tile(cos, (1, 4)), jnp.tile(jnp.concatenate([-sin, sin], axis=1), (1, 2)), cos.T, sin.T


def _pad_cols(w, width):
    return jnp.pad(w, ((0, 0), (0, width - w.shape[1])))


def kernel(x, attn_norm_g, w_qkv_a, lambda_q1, lambda_k1, lambda_q2, lambda_k2, subln_g, w_o_a, kv_in_norm_g, w_dkv, kv_norm_g, w_ukv, w_dq, q_norm_g, w_uq, w_o_b, mlp_norm_g, w_up, w_down, final_norm_g):
    b, s, _ = x.shape
    t = b * s
    cos, sin, cos_t, sin_t = _rope_tables(s)
    row = lambda g: g.reshape(1, -1)
    h = x.reshape(t, D_MODEL)

    lambda_init = 0.8 - 0.6 * math.exp(-0.3 * 0)
    qz, k, vt = _proj_a(h, row(attn_norm_g[0]), w_qkv_a[0].astype(BF16), cos, sin, s)
    lam_rows = [row(p[0]) for p in (lambda_q1, lambda_k1, lambda_q2, lambda_k2)]
    o = _attn_call(functools.partial(_diff_attn_kernel, lambda_init=lambda_init), 2,
                   qz.reshape(b, s, HEADS * SLOT), k.reshape(b, s, D_MODEL), LANES,
                   vt, [subln_g[0].reshape(V_DIM, 1)] + lam_rows, "diff_attn")
    h = _post(h, o.reshape(t, D_MODEL), w_o_a[0].astype(BF16), row(mlp_norm_g[0]),
              w_up[0].astype(BF16), w_down[0].astype(BF16), row(final_norm_g), False, "post_a")

    w_ukv_h = w_ukv.reshape(KV_LORA, HEADS, MLA_NOPE + V_DIM)
    w_uk = w_ukv_h[:, :, :MLA_NOPE].reshape(KV_LORA, HEADS * MLA_NOPE)
    w_uv = w_ukv_h[:, :, MLA_NOPE:].reshape(KV_LORA, HEADS * V_DIM)
    q, kk, vt = _proj_b(h, row(kv_in_norm_g), _pad_cols(w_dkv, KV_LORA + LANES).astype(BF16), row(kv_norm_g),
                       w_uk.astype(BF16), w_uv.astype(BF16), row(attn_norm_g[1]), w_dq[0].astype(BF16),
                       row(q_norm_g[0]), w_uq[0].T.astype(BF16), cos, sin, cos_t, sin_t, s)
    o = _attn_call(_mla_attn_kernel, 1, q, kk.reshape(b, s, HEADS * SLOT), SLOT,
                   vt, [], "mla_attn")
    h = _post(h, o.reshape(t, D_MODEL), w_o_b[0].astype(BF16), row(mlp_norm_g[1]),
              w_up[1].astype(BF16), w_down[1].astype(BF16), row(final_norm_g), True, "post_b")
    return h.reshape(b, s, D_MODEL)
```

```python
import functools
import math

import jax
import jax.numpy as jnp
from jax import lax
from jax.experimental import pallas as pl
from jax.experimental.pallas import tpu as pltpu

D_MODEL = 1024
DEPTH = 2
N_A = DEPTH // 2
HEADS = 8
HEAD_DIM = 64
V_DIM = 128
MLA_NOPE = 128
MLA_ROPE = 64
Q_LORA = 384
KV_LORA = 256
D_FF = 4 * D_MODEL
ROPE_THETA = 10000.0
EPS = 1e-6
SUBLN_EPS = 1e-5

LANES = 128
SLOT = 2 * LANES
TOKEN_TILE = 1024
MLP_TILE = 1024
Q_TILE = 256
K_STEP = 512
Q_SUB = 8
STEPS_PER_SUPER = Q_SUB * Q_TILE // K_STEP
ONES_ROWS = 16
FF_CHUNK = 1024
NEG_BIG = -1e30
LOG2_E = math.log2(math.e)
VMEM_LIMIT = 56 * 1024 * 1024

BF16 = jnp.bfloat16
F32 = jnp.float32
NT_DIMS = (((1,), (1,)), ((), ()))


def _rms(x, g, eps):
    return x * lax.rsqrt(jnp.mean(x * x, axis=-1, keepdims=True) + eps) * g


def _rope_slab(x, cos, sin_signed):
    lane = lax.broadcasted_iota(jnp.int32, x.shape, 1)
    first = (lane % HEAD_DIM) < (HEAD_DIM // 2)
    swapped = jnp.where(first, pltpu.roll(x, LANES - HEAD_DIM // 2, 1), pltpu.roll(x, HEAD_DIM // 2, 1))
    return x * cos + swapped * sin_signed


def _dot(a, b):
    return jnp.dot(a, b, preferred_element_type=F32)


def _values_t_spec(steps_per_seq):
    return pl.BlockSpec((1, HEADS, TOKEN_TILE // K_STEP, V_DIM, K_STEP),
                        lambda i: (i // steps_per_seq, 0, i % steps_per_seq, 0, 0))


def _values_t_shape(tokens, seq):
    return jax.ShapeDtypeStruct((tokens // seq, HEADS, seq // K_STEP, V_DIM, K_STEP), BF16)


def _proj_a_kernel(x_ref, g_ref, w_ref, cos_ref, sin_ref, qz_ref, k_ref, vt_ref):
    hn = _rms(x_ref[...], g_ref[...], EPS).astype(BF16)
    qkv = _dot(hn, w_ref[...])
    cos, sin = cos_ref[...], sin_ref[...]
    lane = lax.broadcasted_iota(jnp.int32, (x_ref.shape[0], LANES), 1)
    comp1 = lane < HEAD_DIM
    scale = HEAD_DIM ** -0.5 * LOG2_E
    for h in range(HEADS):
        q = _rope_slab(qkv[:, h * LANES:(h + 1) * LANES], cos, sin) * scale
        qz_ref[:, h * SLOT:h * SLOT + LANES] = jnp.where(comp1, q, 0.0).astype(BF16)
        qz_ref[:, h * SLOT + LANES:(h + 1) * SLOT] = jnp.where(comp1, 0.0, q).astype(BF16)
        k = _rope_slab(qkv[:, D_MODEL + h * LANES:D_MODEL + (h + 1) * LANES], cos, sin)
        k_ref[:, h * LANES:(h + 1) * LANES] = k.astype(BF16)
        for j in range(TOKEN_TILE // K_STEP):
            v = qkv[j * K_STEP:(j + 1) * K_STEP, 2 * D_MODEL + h * V_DIM:2 * D_MODEL + (h + 1) * V_DIM]
            vt_ref[0, h, j] = v.T.astype(BF16)


def _proj_a(x2, g, w_qkv, cos, sin, seq):
    t = x2.shape[0]
    tm = TOKEN_TILE
    pos_blocks = seq // tm
    const = lambda i: (0, 0)
    return pl.pallas_call(
        _proj_a_kernel,
        grid=(t // tm,),
        in_specs=[
            pl.BlockSpec((tm, D_MODEL), lambda i: (i, 0)),
            pl.BlockSpec((1, D_MODEL), const),
            pl.BlockSpec((D_MODEL, 3 * D_MODEL), const),
            pl.BlockSpec((tm, LANES), lambda i: (i % pos_blocks, 0)),
            pl.BlockSpec((tm, LANES), lambda i: (i % pos_blocks, 0)),
        ],
        out_specs=[
            pl.BlockSpec((tm, HEADS * SLOT), lambda i: (i, 0)),
            pl.BlockSpec((tm, D_MODEL), lambda i: (i, 0)),
            _values_t_spec(pos_blocks),
        ],
        out_shape=[
            jax.ShapeDtypeStruct((t, HEADS * SLOT), BF16),
            jax.ShapeDtypeStruct((t, D_MODEL), BF16),
            _values_t_shape(t, seq),
        ],
        compiler_params=pltpu.CompilerParams(dimension_semantics=("arbitrary",), vmem_limit_bytes=VMEM_LIMIT),
        name="proj_a",
    )(x2, g, w_qkv, cos, sin)


def _causal_keep(offset, n_keys):
    key = lax.broadcasted_iota(jnp.int32, (n_keys, Q_TILE), 0)
    qry = lax.broadcasted_iota(jnp.int32, (n_keys, Q_TILE), 1)
    return key <= qry + offset


def _diag_plan(e):
    plan = []
    for a in range(Q_SUB):
        offset = Q_TILE * a - K_STEP * e
        if offset + Q_TILE > 0:
            plan.append((a, None if offset >= K_STEP - 1 else offset, min(K_STEP, offset + Q_TILE)))
    return plan


def _n_softmax(q_cols):
    return 1 if q_cols is None else len(q_cols)


def _score_chain(q_ref, q_cols, k_ref, s_ref, mx_ref, sup, step, a, c, slot, n_keys=K_STEP):
    k_t = k_ref[0, pl.ds(pl.multiple_of(step * K_STEP, K_STEP), n_keys), :]
    if q_cols is None:
        s_t = _dot(k_t, q_ref[0, 0, sup * Q_SUB + a])
    else:
        rows = pl.ds(pl.multiple_of((sup * Q_SUB + a) * Q_TILE, Q_TILE), Q_TILE)
        s_t = lax.dot_general(k_t, q_ref[0, rows, q_cols[c]], NT_DIMS, preferred_element_type=F32)
    ch = a * _n_softmax(q_cols) + c
    s_ref[slot, ch, :n_keys] = s_t
    mx_ref[slot, ch] = jnp.max(s_t, axis=0, keepdims=True)


def _flash_scores(q_ref, q_cols, k_ref, s_ref, mx_ref, sup, step, plan, slot):
    for a, _, n_keys in plan:
        for c in range(_n_softmax(q_cols)):
            _score_chain(q_ref, q_cols, k_ref, s_ref, mx_ref, sup, step, a, c, slot, n_keys)


EVERYONE = [(a, None, K_STEP) for a in range(Q_SUB)]


def _flash_super_tile(q_ref, q_cols, k_ref, vt_ref, s_ref, mx_ref, m_ref, acc_ref, sup, next_sup, finalize):
    n = _n_softmax(q_cols)
    m_ref[...] = jnp.full(m_ref.shape, NEG_BIG, F32)
    acc_ref[...] = jnp.zeros(acc_ref.shape, F32)
    first_step = sup * STEPS_PER_SUPER
    ones = jnp.ones((ONES_ROWS, K_STEP), BF16)

    def scores(step, plan, slot):
        _flash_scores(q_ref, q_cols, k_ref, s_ref, mx_ref, sup, step, plan, slot)

    def update(step, plan, slot, after_chain=None, before_chain=None):
        vt_aug = jnp.concatenate([vt_ref[0, 0, step], ones], axis=0)
        for a, offset, n_keys in plan:
            for c in range(n):
                ch = a * n + c
                if before_chain is not None:
                    before_chain(a, c)
                if offset is None:
                    s_t, mx = s_ref[slot, ch], mx_ref[slot, ch]
                else:
                    s_t = jnp.where(_causal_keep(offset, n_keys), s_ref[slot, ch, :n_keys], NEG_BIG)
                    mx = jnp.max(s_t, axis=0, keepdims=True)
                m_old = m_ref[ch]
                m_new = jnp.maximum(m_old, mx)
                p = jnp.exp2(s_t - m_new).astype(BF16)
                acc_ref[ch] = jnp.exp2(m_old - m_new) * acc_ref[ch] + _dot(vt_aug[:, :n_keys], p)
                m_ref[ch] = m_new
                if after_chain is not None:
                    after_chain(a, c)

    def pair_body(jj, carry):
        step = 2 * jj
        update(step, EVERYONE, 0, before_chain=functools.partial(
            _score_chain, q_ref, q_cols, k_ref, s_ref, mx_ref, sup, step + 1, slot=1))
        update(step + 1, EVERYONE, 1, before_chain=functools.partial(
            _score_chain, q_ref, q_cols, k_ref, s_ref, mx_ref, sup, step + 2, slot=0))
        return carry

    lax.fori_loop(0, sup * (STEPS_PER_SUPER // 2), pair_body, 0)
    for e in range(STEPS_PER_SUPER):
        plan = _diag_plan(e)[::-1]
        ahead = None
        if e + 1 < STEPS_PER_SUPER:
            keys_next = {a: n_keys for a, _, n_keys in _diag_plan(e + 1)}

            def ahead(a, c, e=e, keys_next=keys_next):
                if a in keys_next:
                    _score_chain(q_ref, q_cols, k_ref, s_ref, mx_ref, sup, first_step + e + 1, a, c, (e + 1) % 2,
                                 keys_next[a])
        refill = None
        if next_sup is not None and e == STEPS_PER_SUPER - 2:
            assert e % 2 == 0
            refill = functools.partial(_score_chain, q_ref, q_cols, k_ref, s_ref, mx_ref, next_sup, 0, slot=0)
            for a in range(Q_SUB):
                if a not in [b for b, _, _ in plan]:
                    for c in range(n):
                        refill(a, c)
        update(first_step + e, plan, e % 2, refill, ahead)
        for a in range(Q_SUB):
            if (Q_TILE * a + Q_TILE - 1) // K_STEP == e:
                finalize(a)


def _run_super_tiles(super_tile, q_ref, q_cols, k_ref, s_ref, mx_ref, n_super):
    order = list(range(n_super))[::-1]
    first = order[0]
    _flash_scores(q_ref, q_cols, k_ref, s_ref, mx_ref, first, 0, EVERYONE if first > 0 else _diag_plan(0), 0)
    for sup, next_sup in zip(order, order[1:] + [None]):
        super_tile(sup, next_sup)


def _softmax_out(acc_ref, ch):
    acc = acc_ref[ch]
    return acc[:V_DIM] * (1.0 / acc[V_DIM:V_DIM + 1])


def _diff_attn_kernel(qz_ref, k_ref, vt_ref, g_ref, lq1_ref, lk1_ref, lq2_ref, lk2_ref, o_ref,
                      s_ref, mx_ref, m_ref, acc_ref, *, lambda_init):
    lam = (jnp.exp(jnp.sum(lq1_ref[...] * lk1_ref[...], keepdims=True))
           - jnp.exp(jnp.sum(lq2_ref[...] * lk2_ref[...], keepdims=True)) + lambda_init)
    g_col = g_ref[...]
    q_cols = [slice(0, LANES), slice(LANES, SLOT)]

    def super_tile(sup, next_sup):
        def finalize(a):
            o_t = _softmax_out(acc_ref, 2 * a) - lam * _softmax_out(acc_ref, 2 * a + 1)
            y_t = o_t * lax.rsqrt(jnp.mean(o_t * o_t, axis=0, keepdims=True) + SUBLN_EPS) * g_col
            rows = pl.ds(pl.multiple_of((sup * Q_SUB + a) * Q_TILE, Q_TILE), Q_TILE)
            o_ref[0, rows, :] = (y_t * (1.0 - lambda_init)).T.astype(BF16)

        _flash_super_tile(qz_ref, q_cols, k_ref, vt_ref, s_ref, mx_ref, m_ref, acc_ref, sup, next_sup, finalize)

    _run_super_tiles(super_tile, qz_ref, q_cols, k_ref, s_ref, mx_ref, o_ref.shape[1] // (Q_TILE * Q_SUB))


def _mla_attn_kernel(q_ref, k_ref, vt_ref, o_ref, s_ref, mx_ref, m_ref, acc_ref):
    q_cols = None

    def super_tile(sup, next_sup):
        def finalize(a):
            rows = pl.ds(pl.multiple_of((sup * Q_SUB + a) * Q_TILE, Q_TILE), Q_TILE)
            o_ref[0, rows, :] = _softmax_out(acc_ref, a).T.astype(BF16)

        _flash_super_tile(q_ref, q_cols, k_ref, vt_ref, s_ref, mx_ref, m_ref, acc_ref, sup, next_sup, finalize)

    _run_super_tiles(super_tile, q_ref, q_cols, k_ref, s_ref, mx_ref, o_ref.shape[1] // (Q_TILE * Q_SUB))


def _attn_call(kernel, n_softmax, q, k, k_width, vt, extra, name):
    b, s, _ = k.shape
    chains = n_softmax * Q_SUB
    small = lambda bi, hi: (0, 0)
    if q.ndim == 3:
        q_spec = pl.BlockSpec((1, s, SLOT), lambda bi, hi: (bi, 0, hi))
    else:
        q_spec = pl.BlockSpec((1, 1) + q.shape[2:], lambda bi, hi: (bi, hi, 0, 0, 0))
    in_specs = [
        q_spec,
        pl.BlockSpec((1, s, k_width), lambda bi, hi: (bi, 0, hi)),
        pl.BlockSpec((1, 1, s // K_STEP, V_DIM, K_STEP), lambda bi, hi: (bi, hi, 0, 0, 0)),
    ] + [pl.BlockSpec(e.shape, small) for e in extra]
    return pl.pallas_call(
        kernel,
        grid=(b, HEADS),
        in_specs=in_specs,
        out_specs=pl.BlockSpec((1, s, V_DIM), lambda bi, hi: (bi, 0, hi)),
        out_shape=jax.ShapeDtypeStruct((b, s, HEADS * V_DIM), BF16),
        scratch_shapes=[
            pltpu.VMEM((2, chains, K_STEP, Q_TILE), F32),
            pltpu.VMEM((2, chains, 1, Q_TILE), F32),
            pltpu.VMEM((chains, 1, Q_TILE), F32),
            pltpu.VMEM((chains, V_DIM + ONES_ROWS, Q_TILE), F32),
        ],
        compiler_params=pltpu.CompilerParams(dimension_semantics=("arbitrary", "arbitrary"),
                                             vmem_limit_bytes=VMEM_LIMIT),
        name=name,
    )(q, k, vt, *extra)


def _post_kernel(h_ref, o_ref, wo_ref, g_ref, wup_ref, wdn_ref, gf_ref, out_ref, *, final_norm):
    h1 = h_ref[...] + _dot(o_ref[...], wo_ref[...])
    hn = _rms(h1, g_ref[...], EPS).astype(BF16)
    acc = h1
    for f in range(D_FF // FF_CHUNK):
        cols = slice(f * FF_CHUNK, (f + 1) * FF_CHUNK)
        u = jnp.maximum(_dot(hn, wup_ref[:, cols]), 0.0)
        acc = acc + _dot((u * u).astype(BF16), wdn_ref[cols, :])
    if final_norm:
        acc = _rms(acc, gf_ref[...], EPS)
    out_ref[...] = acc


def _post(h2, o2, w_o, g_mlp, w_up, w_down, g_final, final_norm, name):
    t = h2.shape[0]
    tm = MLP_TILE
    const = lambda i: (0, 0)
    row = lambda i: (i, 0)
    resident = functools.partial(pl.BlockSpec, index_map=const, pipeline_mode=pl.Buffered(1))
    return pl.pallas_call(
        functools.partial(_post_kernel, final_norm=final_norm),
        grid=(t // tm,),
        in_specs=[
            pl.BlockSpec((tm, D_MODEL), row),
            pl.BlockSpec((tm, D_MODEL), row),
            resident((D_MODEL, D_MODEL)),
            pl.BlockSpec((1, D_MODEL), const),
            resident((D_MODEL, D_FF)),
            resident((D_FF, D_MODEL)),
            pl.BlockSpec((1, D_MODEL), const),
        ],
        out_specs=pl.BlockSpec((tm, D_MODEL), row),
        out_shape=jax.ShapeDtypeStruct((t, D_MODEL), F32),
        compiler_params=pltpu.CompilerParams(dimension_semantics=("arbitrary",), vmem_limit_bytes=VMEM_LIMIT),
        name=name,
    )(h2, o2, w_o, g_mlp, w_up, w_down, g_final)


def _proj_b_kernel(h_ref, gkv_ref, wdkv_ref, gc_ref, wuk_ref, wuv_ref, gq_ref, wdq_ref, gcq_ref, wuqt_ref,
                   cos_ref, sin_ref, cost_ref, sint_ref, q_ref, k_ref, vt_ref):
    h = h_ref[...]
    cos, sin = cos_ref[...], sin_ref[...]
    ckv = _dot(_rms(h, gkv_ref[...], EPS).astype(BF16), wdkv_ref[...])
    cq = _dot(_rms(h, gq_ref[...], EPS).astype(BF16), wdq_ref[...])
    c = _rms(ckv[:, :KV_LORA], gc_ref[...], EPS).astype(BF16)
    k_rope = _rope_slab(ckv[:, KV_LORA:], cos, sin).astype(BF16)
    k_nope = _dot(c, wuk_ref[...]).astype(BF16)
    for hd in range(HEADS):
        k_ref[:, hd * SLOT:hd * SLOT + LANES] = k_nope[:, hd * LANES:(hd + 1) * LANES]
        k_ref[:, hd * SLOT + LANES:(hd + 1) * SLOT] = k_rope
    v = _dot(c, wuv_ref[...])
    for hd in range(HEADS):
        for j in range(TOKEN_TILE // K_STEP):
            vt_ref[0, hd, j] = v[j * K_STEP:(j + 1) * K_STEP, hd * V_DIM:(hd + 1) * V_DIM].T.astype(BF16)
    q_t = _dot(wuqt_ref[...], _rms(cq, gcq_ref[...], EPS).T.astype(BF16))
    cos_t, sin_t = cost_ref[...], sint_ref[...]
    scale = (MLA_NOPE + MLA_ROPE) ** -0.5 * LOG2_E
    half = MLA_ROPE // 2
    pad = jnp.zeros((SLOT - MLA_NOPE - MLA_ROPE, q_t.shape[1]), F32)
    for hd in range(HEADS):
        base = hd * (MLA_NOPE + MLA_ROPE)
        x1 = q_t[base + MLA_NOPE:base + MLA_NOPE + half]
        x2 = q_t[base + MLA_NOPE + half:base + MLA_NOPE + MLA_ROPE]
        tile = jnp.concatenate([q_t[base:base + MLA_NOPE], x1 * cos_t - x2 * sin_t, x2 * cos_t + x1 * sin_t, pad],
                               axis=0) * scale
        for j in range(q_t.shape[1] // Q_TILE):
            q_ref[0, hd, j] = tile[:, j * Q_TILE:(j + 1) * Q_TILE].astype(BF16)


def _proj_b(h2, g_kv_in, w_dkv, g_c, w_uk, w_uv, g_q, w_dq, g_cq, w_uq_t, cos, sin, cos_t, sin_t, seq):
    t = h2.shape[0]
    tm = TOKEN_TILE
    pos_blocks = seq // tm
    const = lambda i: (0, 0)
    row = lambda i: (i, 0)
    full = lambda a: pl.BlockSpec(a.shape, const)
    return pl.pallas_call(
        _proj_b_kernel,
        grid=(t // tm,),
        in_specs=[
            pl.BlockSpec((tm, D_MODEL), row),
            full(g_kv_in), full(w_dkv), full(g_c), full(w_uk), full(w_uv),
            full(g_q), full(w_dq), full(g_cq), full(w_uq_t),
            pl.BlockSpec((tm, LANES), lambda i: (i % pos_blocks, 0)),
            pl.BlockSpec((tm, LANES), lambda i: (i % pos_blocks, 0)),
            pl.BlockSpec((MLA_ROPE // 2, tm), lambda i: (0, i % pos_blocks)),
            pl.BlockSpec((MLA_ROPE // 2, tm), lambda i: (0, i % pos_blocks)),
        ],
        out_specs=[
            pl.BlockSpec((1, HEADS, tm // Q_TILE, SLOT, Q_TILE), lambda i: (i // pos_blocks, 0, i % pos_blocks, 0, 0)),
            pl.BlockSpec((tm, HEADS * SLOT), row),
            _values_t_spec(pos_blocks),
        ],
        out_shape=[
            jax.ShapeDtypeStruct((t // seq, HEADS, seq // Q_TILE, SLOT, Q_TILE), BF16),
            jax.ShapeDtypeStruct((t, HEADS * SLOT), BF16),
            _values_t_shape(t, seq),
        ],
        compiler_params=pltpu.CompilerParams(dimension_semantics=("arbitrary",), vmem_limit_bytes=VMEM_LIMIT),
        name="proj_b",
    )(h2, g_kv_in, w_dkv, g_c, w_uk, w_uv, g_q, w_dq, g_cq, w_uq_t, cos, sin, cos_t, sin_t)


def _rope_tables(seq):
    pos = jnp.arange(seq, dtype=F32)
    inv_freq = ROPE_THETA ** (-jnp.arange(0, HEAD_DIM, 2, dtype=F32) / HEAD_DIM)
    ang = pos[:, None] * inv_freq[None, :]
    cos, sin = jnp.cos(ang), jnp.sin(ang)
    return jnp.tile(cos, (1, 4)), jnp.tile(jnp.concatenate([-sin, sin], axis=1), (1, 2)), cos.T, sin.T


def _pad_cols(w, width):
    return jnp.pad(w, ((0, 0), (0, width - w.shape[1])))


def kernel(x, attn_norm_g, w_qkv_a, lambda_q1, lambda_k1, lambda_q2, lambda_k2, subln_g, w_o_a, kv_in_norm_g, w_dkv, kv_norm_g, w_ukv, w_dq, q_norm_g, w_uq, w_o_b, mlp_norm_g, w_up, w_down, final_norm_g):
    b, s, _ = x.shape
    t = b * s
    cos, sin, cos_t, sin_t = _rope_tables(s)
    row = lambda g: g.reshape(1, -1)
    h = x.reshape(t, D_MODEL)

    lambda_init = 0.8 - 0.6 * math.exp(-0.3 * 0)
    qz, k, vt = _proj_a(h, row(attn_norm_g[0]), w_qkv_a[0].astype(BF16), cos, sin, s)
    lam_rows = [row(p[0]) for p in (lambda_q1, lambda_k1, lambda_q2, lambda_k2)]
    o = _attn_call(functools.partial(_diff_attn_kernel, lambda_init=lambda_init), 2,
                   qz.reshape(b, s, HEADS * SLOT), k.reshape(b, s, D_MODEL), LANES,
                   vt, [subln_g[0].reshape(V_DIM, 1)] + lam_rows, "diff_attn")
    h = _post(h, o.reshape(t, D_MODEL), w_o_a[0].astype(BF16), row(mlp_norm_g[0]),
              w_up[0].astype(BF16), w_down[0].astype(BF16), row(final_norm_g), False, "post_a")

    w_ukv_h = w_ukv.reshape(KV_LORA, HEADS, MLA_NOPE + V_DIM)
    w_uk = w_ukv_h[:, :, :MLA_NOPE].reshape(KV_LORA, HEADS * MLA_NOPE)
    w_uv = w_ukv_h[:, :, MLA_NOPE:].reshape(KV_LORA, HEADS * V_DIM)
    q, kk, vt = _proj_b(h, row(kv_in_norm_g), _pad_cols(w_dkv, KV_LORA + LANES).astype(BF16), row(kv_norm_g),
                       w_uk.astype(BF16), w_uv.astype(BF16), row(attn_norm_g[1]), w_dq[0].astype(BF16),
                       row(q_norm_g[0]), w_uq[0].T.astype(BF16), cos, sin, cos_t, sin_t, s)
    o = _attn_call(_mla_attn_kernel, 1, q, kk.reshape(b, s, HEADS * SLOT), SLOT,
                   vt, [], "mla_attn")
    h = _post(h, o.reshape(t, D_MODEL), w_o_b[0].astype(BF16), row(mlp_norm_g[1]),
              w_up[1].astype(BF16), w_down[1].astype(BF16), row(final_norm_g), True, "post_b")
    return h.reshape(b, s, D_MODEL)
```

```python
import functools
import math

import jax
import jax.numpy as jnp
from jax import lax
from jax.experimental import pallas as pl
from jax.experimental.pallas import tpu as pltpu

D_MODEL = 1024
HEADS = 8
HEAD_DIM = 64
V_DIM = 128
MLA_NOPE = 128
MLA_ROPE = 64
Q_LORA = 384
KV_LORA = 256
D_FF = 4 * D_MODEL
ROPE_THETA = 10000.0
EPS = 1e-6
SUBLN_EPS = 1e-5

LANES = 128
SLOT = 2 * LANES
TOKEN_TILE = 1024
MLP_TILE = 1024
Q_TILE = 256
K_STEP = 512
Q_SUB = 8
STEPS_PER_SUPER = Q_SUB * Q_TILE // K_STEP
ONES_ROWS = 16
FF_CHUNK = 1024
NEG_BIG = -1e30
LOG2_E = math.log2(math.e)
VMEM_LIMIT = 56 * 1024 * 1024

BF16 = jnp.bfloat16
F32 = jnp.float32
NT_DIMS = (((1,), (1,)), ((), ()))


def _rms(x, g, eps):
    return x * lax.rsqrt(jnp.mean(x * x, axis=-1, keepdims=True) + eps) * g


def _rope_slab(x, cos, sin_signed):
    lane = lax.broadcasted_iota(jnp.int32, x.shape, 1)
    first = (lane % HEAD_DIM) < (HEAD_DIM // 2)
    swapped = jnp.where(first, pltpu.roll(x, LANES - HEAD_DIM // 2, 1), pltpu.roll(x, HEAD_DIM // 2, 1))
    return x * cos + swapped * sin_signed


def _dot(a, b):
    return jnp.dot(a, b, preferred_element_type=F32)


def _values_t_spec(steps_per_seq):
    return pl.BlockSpec((1, HEADS, TOKEN_TILE // K_STEP, V_DIM, K_STEP),
                        lambda i: (i // steps_per_seq, 0, i % steps_per_seq, 0, 0))


def _values_t_shape(tokens, seq):
    return jax.ShapeDtypeStruct((tokens // seq, HEADS, seq // K_STEP, V_DIM, K_STEP), BF16)


def _proj_a_kernel(x_ref, g_ref, w_ref, cos_ref, sin_ref, qz_ref, k_ref, vt_ref):
    hn = _rms(x_ref[...], g_ref[...], EPS).astype(BF16)
    qkv = _dot(hn, w_ref[...])
    cos, sin = cos_ref[...], sin_ref[...]
    lane = lax.broadcasted_iota(jnp.int32, (x_ref.shape[0], LANES), 1)
    comp1 = lane < HEAD_DIM
    scale = HEAD_DIM ** -0.5 * LOG2_E
    for h in range(HEADS):
        q = _rope_slab(qkv[:, h * LANES:(h + 1) * LANES], cos, sin) * scale
        qz_ref[:, h * SLOT:h * SLOT + LANES] = jnp.where(comp1, q, 0.0).astype(BF16)
        qz_ref[:, h * SLOT + LANES:(h + 1) * SLOT] = jnp.where(comp1, 0.0, q).astype(BF16)
        k = _rope_slab(qkv[:, D_MODEL + h * LANES:D_MODEL + (h + 1) * LANES], cos, sin)
        k_ref[:, h * LANES:(h + 1) * LANES] = k.astype(BF16)
        for j in range(TOKEN_TILE // K_STEP):
            v = qkv[j * K_STEP:(j + 1) * K_STEP, 2 * D_MODEL + h * V_DIM:2 * D_MODEL + (h + 1) * V_DIM]
            vt_ref[0, h, j] = v.T.astype(BF16)


def _proj_a(x2, g, w_qkv, cos, sin, seq):
    t = x2.shape[0]
    tm = TOKEN_TILE
    pos_blocks = seq // tm
    const = lambda i: (0, 0)
    return pl.pallas_call(
        _proj_a_kernel,
        grid=(t // tm,),
        in_specs=[
            pl.BlockSpec((tm, D_MODEL), lambda i: (i, 0)),
            pl.BlockSpec((1, D_MODEL), const),
            pl.BlockSpec((D_MODEL, 3 * D_MODEL), const),
            pl.BlockSpec((tm, LANES), lambda i: (i % pos_blocks, 0)),
            pl.BlockSpec((tm, LANES), lambda i: (i % pos_blocks, 0)),
        ],
        out_specs=[
            pl.BlockSpec((tm, HEADS * SLOT), lambda i: (i, 0)),
            pl.BlockSpec((tm, D_MODEL), lambda i: (i, 0)),
            _values_t_spec(pos_blocks),
        ],
        out_shape=[
            jax.ShapeDtypeStruct((t, HEADS * SLOT), BF16),
            jax.ShapeDtypeStruct((t, D_MODEL), BF16),
            _values_t_shape(t, seq),
        ],
        compiler_params=pltpu.CompilerParams(dimension_semantics=("arbitrary",), vmem_limit_bytes=VMEM_LIMIT),
        name="proj_a",
    )(x2, g, w_qkv, cos, sin)


def _causal_keep(offset, n_keys):
    key = lax.broadcasted_iota(jnp.int32, (n_keys, Q_TILE), 0)
    qry = lax.broadcasted_iota(jnp.int32, (n_keys, Q_TILE), 1)
    return key <= qry + offset


def _diag_plan(e):
    plan = []
    for a in range(Q_SUB):
        offset = Q_TILE * a - K_STEP * e
        if offset + Q_TILE > 0:
            plan.append((a, None if offset >= K_STEP - 1 else offset, min(K_STEP, offset + Q_TILE)))
    return plan


def _n_softmax(q_cols):
    return 1 if q_cols is None else len(q_cols)


def _score_chain(q_ref, q_cols, k_ref, s_ref, mx_ref, sup, step, a, c, slot, n_keys=K_STEP):
    k_t = k_ref[0, pl.ds(pl.multiple_of(step * K_STEP, K_STEP), n_keys), :]
    if q_cols is None:
        s_t = _dot(k_t, q_ref[0, 0, sup * Q_SUB + a])
    else:
        rows = pl.ds(pl.multiple_of((sup * Q_SUB + a) * Q_TILE, Q_TILE), Q_TILE)
        s_t = lax.dot_general(k_t, q_ref[0, rows, q_cols[c]], NT_DIMS, preferred_element_type=F32)
    ch = a * _n_softmax(q_cols) + c
    s_ref[slot, ch, :n_keys] = s_t
    mx_ref[slot, ch] = jnp.max(s_t, axis=0, keepdims=True)


def _flash_scores(q_ref, q_cols, k_ref, s_ref, mx_ref, sup, step, plan, slot):
    for a, _, n_keys in plan:
        for c in range(_n_softmax(q_cols)):
            _score_chain(q_ref, q_cols, k_ref, s_ref, mx_ref, sup, step, a, c, slot, n_keys)


EVERYONE = [(a, None, K_STEP) for a in range(Q_SUB)]


def _flash_super_tile(q_ref, q_cols, k_ref, vt_ref, s_ref, mx_ref, m_ref, acc_ref, sup, next_sup, finalize):
    n = _n_softmax(q_cols)
    m_ref[...] = jnp.full(m_ref.shape, NEG_BIG, F32)
    acc_ref[...] = jnp.zeros(acc_ref.shape, F32)
    first_step = sup * STEPS_PER_SUPER
    ones = jnp.ones((ONES_ROWS, K_STEP), BF16)

    def update(step, plan, slot, after_chain=None, before_chain=None):
        vt_aug = jnp.concatenate([vt_ref[0, 0, step], ones], axis=0)
        for a, offset, n_keys in plan:
            for c in range(n):
                ch = a * n + c
                if before_chain is not None:
                    before_chain(a, c)
                if offset is None:
                    s_t, mx = s_ref[slot, ch], mx_ref[slot, ch]
                else:
                    s_t = jnp.where(_causal_keep(offset, n_keys), s_ref[slot, ch, :n_keys], NEG_BIG)
                    mx = jnp.max(s_t, axis=0, keepdims=True)
                m_old = m_ref[ch]
                m_new = jnp.maximum(m_old, mx)
                p = jnp.exp2(s_t - m_new).astype(BF16)
                acc_ref[ch] = jnp.exp2(m_old - m_new) * acc_ref[ch] + _dot(vt_aug[:, :n_keys], p)
                m_ref[ch] = m_new
                if after_chain is not None:
                    after_chain(a, c)

    def pair_body(jj, carry):
        step = 2 * jj
        update(step, EVERYONE, 0, before_chain=functools.partial(
            _score_chain, q_ref, q_cols, k_ref, s_ref, mx_ref, sup, step + 1, slot=1))
        update(step + 1, EVERYONE, 1, before_chain=functools.partial(
            _score_chain, q_ref, q_cols, k_ref, s_ref, mx_ref, sup, step + 2, slot=0))
        return carry

    lax.fori_loop(0, sup * (STEPS_PER_SUPER // 2), pair_body, 0)
    for e in range(STEPS_PER_SUPER):
        plan = _diag_plan(e)[::-1]
        ahead = None
        if e + 1 < STEPS_PER_SUPER:
            keys_next = {a: n_keys for a, _, n_keys in _diag_plan(e + 1)}

            def ahead(a, c, e=e, keys_next=keys_next):
                if a in keys_next:
                    _score_chain(q_ref, q_cols, k_ref, s_ref, mx_ref, sup, first_step + e + 1, a, c, (e + 1) % 2,
                                 keys_next[a])
        refill = None
        if next_sup is not None and e == STEPS_PER_SUPER - 2:
            assert e % 2 == 0
            refill = functools.partial(_score_chain, q_ref, q_cols, k_ref, s_ref, mx_ref, next_sup, 0, slot=0)
            for a in range(Q_SUB):
                if a not in [b for b, _, _ in plan]:
                    for c in range(n):
                        refill(a, c)
        update(first_step + e, plan, e % 2, refill, ahead)
        for a in range(Q_SUB):
            if (Q_TILE * a + Q_TILE - 1) // K_STEP == e:
                finalize(a)


def _run_super_tiles(super_tile, q_ref, q_cols, k_ref, s_ref, mx_ref, n_super):
    order = list(range(n_super))[::-1]
    first = order[0]
    _flash_scores(q_ref, q_cols, k_ref, s_ref, mx_ref, first, 0, EVERYONE if first > 0 else _diag_plan(0), 0)
    for sup, next_sup in zip(order, order[1:] + [None]):
        super_tile(sup, next_sup)


def _softmax_out(acc_ref, ch):
    acc = acc_ref[ch]
    return acc[:V_DIM] * (1.0 / acc[V_DIM:V_DIM + 1])


def _diff_attn_kernel(qz_ref, k_ref, vt_ref, g_ref, lq1_ref, lk1_ref, lq2_ref, lk2_ref, o_ref,
                      s_ref, mx_ref, m_ref, acc_ref, *, lambda_init):
    lam = (jnp.exp(jnp.sum(lq1_ref[...] * lk1_ref[...], keepdims=True))
           - jnp.exp(jnp.sum(lq2_ref[...] * lk2_ref[...], keepdims=True)) + lambda_init)
    g_col = g_ref[...]
    q_cols = [slice(0, LANES), slice(LANES, SLOT)]

    def super_tile(sup, next_sup):
        def finalize(a):
            o_t = _softmax_out(acc_ref, 2 * a) - lam * _softmax_out(acc_ref, 2 * a + 1)
            y_t = o_t * lax.rsqrt(jnp.mean(o_t * o_t, axis=0, keepdims=True) + SUBLN_EPS) * g_col
            rows = pl.ds(pl.multiple_of((sup * Q_SUB + a) * Q_TILE, Q_TILE), Q_TILE)
            o_ref[0, rows, :] = (y_t * (1.0 - lambda_init)).T.astype(BF16)

        _flash_super_tile(qz_ref, q_cols, k_ref, vt_ref, s_ref, mx_ref, m_ref, acc_ref, sup, next_sup, finalize)

    _run_super_tiles(super_tile, qz_ref, q_cols, k_ref, s_ref, mx_ref, o_ref.shape[1] // (Q_TILE * Q_SUB))


def _mla_attn_kernel(q_ref, k_ref, vt_ref, o_ref, s_ref, mx_ref, m_ref, acc_ref):
    q_cols = None

    def super_tile(sup, next_sup):
        def finalize(a):
            rows = pl.ds(pl.multiple_of((sup * Q_SUB + a) * Q_TILE, Q_TILE), Q_TILE)
            o_ref[0, rows, :] = _softmax_out(acc_ref, a).T.astype(BF16)

        _flash_super_tile(q_ref, q_cols, k_ref, vt_ref, s_ref, mx_ref, m_ref, acc_ref, sup, next_sup, finalize)

    _run_super_tiles(super_tile, q_ref, q_cols, k_ref, s_ref, mx_ref, o_ref.shape[1] // (Q_TILE * Q_SUB))


def _attn_call(kernel, n_softmax, q, k, k_width, vt, extra, name):
    b, s, _ = k.shape
    chains = n_softmax * Q_SUB
    small = lambda bi, hi: (0, 0)
    if q.ndim == 3:
        q_spec = pl.BlockSpec((1, s, SLOT), lambda bi, hi: (bi, 0, hi))
    else:
        q_spec = pl.BlockSpec((1, 1) + q.shape[2:], lambda bi, hi: (bi, hi, 0, 0, 0))
    in_specs = [
        q_spec,
        pl.BlockSpec((1, s, k_width), lambda bi, hi: (bi, 0, hi)),
        pl.BlockSpec((1, 1, s // K_STEP, V_DIM, K_STEP), lambda bi, hi: (bi, hi, 0, 0, 0)),
    ] + [pl.BlockSpec(e.shape, small) for e in extra]
    return pl.pallas_call(
        kernel,
        grid=(b, HEADS),
        in_specs=in_specs,
        out_specs=pl.BlockSpec((1, s, V_DIM), lambda bi, hi: (bi, 0, hi)),
        out_shape=jax.ShapeDtypeStruct((b, s, HEADS * V_DIM), BF16),
        scratch_shapes=[
            pltpu.VMEM((2, chains, K_STEP, Q_TILE), F32),
            pltpu.VMEM((2, chains, 1, Q_TILE), F32),
            pltpu.VMEM((chains, 1, Q_TILE), F32),
            pltpu.VMEM((chains, V_DIM + ONES_ROWS, Q_TILE), F32),
        ],
        compiler_params=pltpu.CompilerParams(dimension_semantics=("arbitrary", "arbitrary"),
                                             vmem_limit_bytes=VMEM_LIMIT),
        name=name,
    )(q, k, vt, *extra)


def _post_kernel(h_ref, o_ref, wo_ref, g_ref, wup_ref, wdn_ref, gf_ref, out_ref, *, final_norm):
    h1 = h_ref[...] + _dot(o_ref[...], wo_ref[...])
    hn = _rms(h1, g_ref[...], EPS).astype(BF16)
    acc = h1
    for f in range(D_FF // FF_CHUNK):
        cols = slice(f * FF_CHUNK, (f + 1) * FF_CHUNK)
        u = jnp.maximum(_dot(hn, wup_ref[:, cols]), 0.0)
        acc = acc + _dot((u * u).astype(BF16), wdn_ref[cols, :])
    if final_norm:
        acc = _rms(acc, gf_ref[...], EPS)
    out_ref[...] = acc


def _post(h2, o2, w_o, g_mlp, w_up, w_down, g_final, final_norm, name):
    t = h2.shape[0]
    tm = MLP_TILE
    const = lambda i: (0, 0)
    row = lambda i: (i, 0)
    resident = functools.partial(pl.BlockSpec, index_map=const, pipeline_mode=pl.Buffered(1))
    return pl.pallas_call(
        functools.partial(_post_kernel, final_norm=final_norm),
        grid=(t // tm,),
        in_specs=[
            pl.BlockSpec((tm, D_MODEL), row),
            pl.BlockSpec((tm, D_MODEL), row),
            resident((D_MODEL, D_MODEL)),
            pl.BlockSpec((1, D_MODEL), const),
            resident((D_MODEL, D_FF)),
            resident((D_FF, D_MODEL)),
            pl.BlockSpec((1, D_MODEL), const),
        ],
        out_specs=pl.BlockSpec((tm, D_MODEL), row),
        out_shape=jax.ShapeDtypeStruct((t, D_MODEL), F32),
        compiler_params=pltpu.CompilerParams(dimension_semantics=("arbitrary",), vmem_limit_bytes=VMEM_LIMIT),
        name=name,
    )(h2, o2, w_o, g_mlp, w_up, w_down, g_final)


def _proj_b_kernel(h_ref, gkv_ref, wdkv_ref, gc_ref, wukv_ref, gq_ref, wdq_ref, gcq_ref, wuqt_ref,
                   cos_ref, sin_ref, cost_ref, sint_ref, q_ref, k_ref, vt_ref):
    h = h_ref[...]
    cos, sin = cos_ref[...], sin_ref[...]
    ckv = _dot(_rms(h, gkv_ref[...], EPS).astype(BF16), wdkv_ref[...])
    cq = _dot(_rms(h, gq_ref[...], EPS).astype(BF16), wdq_ref[...])
    c = _rms(ckv[:, :KV_LORA], gc_ref[...], EPS).astype(BF16)
    k_rope = _rope_slab(ckv[:, KV_LORA:], cos, sin).astype(BF16)
    kv = _dot(c, wukv_ref[...])
    kv_head = MLA_NOPE + V_DIM
    for hd in range(HEADS):
        k_ref[:, hd * SLOT:hd * SLOT + LANES] = kv[:, hd * kv_head:hd * kv_head + MLA_NOPE].astype(BF16)
        k_ref[:, hd * SLOT + LANES:(hd + 1) * SLOT] = k_rope
    for hd in range(HEADS):
        for j in range(TOKEN_TILE // K_STEP):
            v = kv[j * K_STEP:(j + 1) * K_STEP, hd * kv_head + MLA_NOPE:(hd + 1) * kv_head]
            vt_ref[0, hd, j] = v.T.astype(BF16)
    q_t = _dot(wuqt_ref[...], _rms(cq, gcq_ref[...], EPS).T.astype(BF16))
    cos_t, sin_t = cost_ref[...], sint_ref[...]
    scale = (MLA_NOPE + MLA_ROPE) ** -0.5 * LOG2_E
    half = MLA_ROPE // 2
    pad = jnp.zeros((SLOT - MLA_NOPE - MLA_ROPE, q_t.shape[1]), F32)
    for hd in range(HEADS):
        base = hd * (MLA_NOPE + MLA_ROPE)
        x1 = q_t[base + MLA_NOPE:base + MLA_NOPE + half]
        x2 = q_t[base + MLA_NOPE + half:base + MLA_NOPE + MLA_ROPE]
        tile = jnp.concatenate([q_t[base:base + MLA_NOPE], x1 * cos_t - x2 * sin_t, x2 * cos_t + x1 * sin_t, pad],
                               axis=0) * scale
        for j in range(q_t.shape[1] // Q_TILE):
            q_ref[0, hd, j] = tile[:, j * Q_TILE:(j + 1) * Q_TILE].astype(BF16)


def _proj_b(h2, g_kv_in, w_dkv, g_c, w_ukv, g_q, w_dq, g_cq, w_uq_t, cos, sin, cos_t, sin_t, seq):
    t = h2.shape[0]
    tm = TOKEN_TILE
    pos_blocks = seq // tm
    const = lambda i: (0, 0)
    row = lambda i: (i, 0)
    full = lambda a: pl.BlockSpec(a.shape, const)
    return pl.pallas_call(
        _proj_b_kernel,
        grid=(t // tm,),
        in_specs=[
            pl.BlockSpec((tm, D_MODEL), row),
            full(g_kv_in), full(w_dkv), full(g_c), full(w_ukv),
            full(g_q), full(w_dq), full(g_cq), full(w_uq_t),
            pl.BlockSpec((tm, LANES), lambda i: (i % pos_blocks, 0)),
            pl.BlockSpec((tm, LANES), lambda i: (i % pos_blocks, 0)),
            pl.BlockSpec((MLA_ROPE // 2, tm), lambda i: (0, i % pos_blocks)),
            pl.BlockSpec((MLA_ROPE // 2, tm), lambda i: (0, i % pos_blocks)),
        ],
        out_specs=[
            pl.BlockSpec((1, HEADS, tm // Q_TILE, SLOT, Q_TILE), lambda i: (i // pos_blocks, 0, i % pos_blocks, 0, 0)),
            pl.BlockSpec((tm, HEADS * SLOT), row),
            _values_t_spec(pos_blocks),
        ],
        out_shape=[
            jax.ShapeDtypeStruct((t // seq, HEADS, seq // Q_TILE, SLOT, Q_TILE), BF16),
            jax.ShapeDtypeStruct((t, HEADS * SLOT), BF16),
            _values_t_shape(t, seq),
        ],
        compiler_params=pltpu.CompilerParams(dimension_semantics=("arbitrary",), vmem_limit_bytes=VMEM_LIMIT),
        name="proj_b",
    )(h2, g_kv_in, w_dkv, g_c, w_ukv, g_q, w_dq, g_cq, w_uq_t, cos, sin, cos_t, sin_t)


def _rope_tables(seq):
    pos = jnp.arange(seq, dtype=F32)
    inv_freq = ROPE_THETA ** (-jnp.arange(0, HEAD_DIM, 2, dtype=F32) / HEAD_DIM)
    ang = pos[:, None] * inv_freq[None, :]
    cos, sin = jnp.cos(ang), jnp.sin(ang)
    return jnp.tile(cos, (1, 4)), jnp.tile(jnp.concatenate([-sin, sin], axis=1), (1, 2)), cos.T, sin.T


def _pad_cols(w, width):
    return jnp.pad(w, ((0, 0), (0, width - w.shape[1])))


def kernel(x, attn_norm_g, w_qkv_a, lambda_q1, lambda_k1, lambda_q2, lambda_k2, subln_g, w_o_a, kv_in_norm_g, w_dkv, kv_norm_g, w_ukv, w_dq, q_norm_g, w_uq, w_o_b, mlp_norm_g, w_up, w_down, final_norm_g):
    b, s, _ = x.shape
    t = b * s
    cos, sin, cos_t, sin_t = _rope_tables(s)
    row = lambda g: g.reshape(1, -1)
    h = x.reshape(t, D_MODEL)

    lambda_init = 0.8 - 0.6 * math.exp(-0.3 * 0)
    qz, k, vt = _proj_a(h, row(attn_norm_g[0]), w_qkv_a[0].astype(BF16), cos, sin, s)
    lam_rows = [row(p[0]) for p in (lambda_q1, lambda_k1, lambda_q2, lambda_k2)]
    o = _attn_call(functools.partial(_diff_attn_kernel, lambda_init=lambda_init), 2,
                   qz.reshape(b, s, HEADS * SLOT), k.reshape(b, s, D_MODEL), LANES,
                   vt, [subln_g[0].reshape(V_DIM, 1)] + lam_rows, "diff_attn")
    h = _post(h, o.reshape(t, D_MODEL), w_o_a[0].astype(BF16), row(mlp_norm_g[0]),
              w_up[0].astype(BF16), w_down[0].astype(BF16), row(final_norm_g), False, "post_a")

    q, kk, vt = _proj_b(h, row(kv_in_norm_g), _pad_cols(w_dkv, KV_LORA + LANES).astype(BF16), row(kv_norm_g),
                       w_ukv.astype(BF16), row(attn_norm_g[1]), w_dq[0].astype(BF16),
                       row(q_norm_g[0]), w_uq[0].T.astype(BF16), cos, sin, cos_t, sin_t, s)
    o = _attn_call(_mla_attn_kernel, 1, q, kk.reshape(b, s, HEADS * SLOT), SLOT,
                   vt, [], "mla_attn")
    h = _post(h, o.reshape(t, D_MODEL), w_o_b[0].astype(BF16), row(mlp_norm_g[1]),
              w_up[1].astype(BF16), w_down[1].astype(BF16), row(final_norm_g), True, "post_b")
    return h.reshape(b, s, D_MODEL)
```

```python
import functools
import math

import jax
import jax.numpy as jnp
from jax import lax
from jax.experimental import pallas as pl
from jax.experimental.pallas import tpu as pltpu

D_MODEL = 1024
HEADS = 8
HEAD_DIM = 64
V_DIM = 128
MLA_NOPE = 128
MLA_ROPE = 64
Q_LORA = 384
KV_LORA = 256
D_FF = 4 * D_MODEL
ROPE_THETA = 10000.0
EPS = 1e-6
SUBLN_EPS = 1e-5

LANES = 128
SLOT = 2 * LANES
TOKEN_TILE = 1024
MLP_TILE = 1024
Q_TILE = 256
K_STEP = 512
Q_SUB = 8
STEPS_PER_SUPER = Q_SUB * Q_TILE // K_STEP
ONES_ROWS = 16
FF_CHUNK = 1024
NEG_BIG = -1e30
LOG2_E = math.log2(math.e)
VMEM_LIMIT = 56 * 1024 * 1024

BF16 = jnp.bfloat16
F32 = jnp.float32
NT_DIMS = (((1,), (1,)), ((), ()))


def _rms(x, g, eps):
    return x * lax.rsqrt(jnp.mean(x * x, axis=-1, keepdims=True) + eps) * g


def _rope_slab(x, cos, sin_signed):
    lane = lax.broadcasted_iota(jnp.int32, x.shape, 1)
    first = (lane % HEAD_DIM) < (HEAD_DIM // 2)
    swapped = jnp.where(first, pltpu.roll(x, LANES - HEAD_DIM // 2, 1), pltpu.roll(x, HEAD_DIM // 2, 1))
    return x * cos + swapped * sin_signed


def _dot(a, b):
    return jnp.dot(a, b, preferred_element_type=F32)


def _values_t_spec(steps_per_seq):
    return pl.BlockSpec((1, HEADS, TOKEN_TILE // K_STEP, V_DIM, K_STEP),
                        lambda i: (i // steps_per_seq, 0, i % steps_per_seq, 0, 0))


def _values_t_shape(tokens, seq):
    return jax.ShapeDtypeStruct((tokens // seq, HEADS, seq // K_STEP, V_DIM, K_STEP), BF16)


def _proj_a_kernel(x_ref, g_ref, w_ref, cos_ref, sin_ref, qz_ref, k_ref, vt_ref):
    hn = _rms(x_ref[...], g_ref[...], EPS).astype(BF16)
    qkv = _dot(hn, w_ref[...])
    cos, sin = cos_ref[...], sin_ref[...]
    lane = lax.broadcasted_iota(jnp.int32, (x_ref.shape[0], LANES), 1)
    comp1 = lane < HEAD_DIM
    scale = HEAD_DIM ** -0.5 * LOG2_E
    for h in range(HEADS):
        q = _rope_slab(qkv[:, h * LANES:(h + 1) * LANES], cos, sin) * scale
        qz_ref[:, h * SLOT:h * SLOT + LANES] = jnp.where(comp1, q, 0.0).astype(BF16)
        qz_ref[:, h * SLOT + LANES:(h + 1) * SLOT] = jnp.where(comp1, 0.0, q).astype(BF16)
        k = _rope_slab(qkv[:, D_MODEL + h * LANES:D_MODEL + (h + 1) * LANES], cos, sin)
        k_ref[:, h * LANES:(h + 1) * LANES] = k.astype(BF16)
        for j in range(TOKEN_TILE // K_STEP):
            v = qkv[j * K_STEP:(j + 1) * K_STEP, 2 * D_MODEL + h * V_DIM:2 * D_MODEL + (h + 1) * V_DIM]
            vt_ref[0, h, j] = v.T.astype(BF16)


def _proj_a(x2, g, w_qkv, cos, sin, seq):
    t = x2.shape[0]
    tm = TOKEN_TILE
    pos_blocks = seq // tm
    const = lambda i: (0, 0)
    return pl.pallas_call(
        _proj_a_kernel,
        grid=(t // tm,),
        in_specs=[
            pl.BlockSpec((tm, D_MODEL), lambda i: (i, 0)),
            pl.BlockSpec((1, D_MODEL), const),
            pl.BlockSpec((D_MODEL, 3 * D_MODEL), const),
            pl.BlockSpec((tm, LANES), lambda i: (i % pos_blocks, 0)),
            pl.BlockSpec((tm, LANES), lambda i: (i % pos_blocks, 0)),
        ],
        out_specs=[
            pl.BlockSpec((tm, HEADS * SLOT), lambda i: (i, 0)),
            pl.BlockSpec((tm, D_MODEL), lambda i: (i, 0)),
            _values_t_spec(pos_blocks),
        ],
        out_shape=[
            jax.ShapeDtypeStruct((t, HEADS * SLOT), BF16),
            jax.ShapeDtypeStruct((t, D_MODEL), BF16),
            _values_t_shape(t, seq),
        ],
        compiler_params=pltpu.CompilerParams(dimension_semantics=("arbitrary",), vmem_limit_bytes=VMEM_LIMIT),
        name="proj_a",
    )(x2, g, w_qkv, cos, sin)


def _causal_keep(offset, n_keys):
    key = lax.broadcasted_iota(jnp.int32, (n_keys, Q_TILE), 0)
    qry = lax.broadcasted_iota(jnp.int32, (n_keys, Q_TILE), 1)
    return key <= qry + offset


def _diag_plan(e):
    plan = []
    for a in range(Q_SUB):
        offset = Q_TILE * a - K_STEP * e
        if offset + Q_TILE > 0:
            plan.append((a, None if offset >= K_STEP - 1 else offset, min(K_STEP, offset + Q_TILE)))
    return plan


def _load_keys(k_ref, rows):
    if isinstance(k_ref, tuple):
        return jnp.concatenate([r[0, rows, :] for r in k_ref], axis=1)
    return k_ref[0, rows, :]


def _n_softmax(q_cols):
    return 1 if q_cols is None else len(q_cols)


def _score_chain(q_ref, q_cols, k_ref, s_ref, mx_ref, sup, step, a, c, slot, n_keys=K_STEP):
    k_t = _load_keys(k_ref, pl.ds(pl.multiple_of(step * K_STEP, K_STEP), n_keys))
    if q_cols is None:
        s_t = _dot(k_t, q_ref[0, 0, sup * Q_SUB + a])
    else:
        rows = pl.ds(pl.multiple_of((sup * Q_SUB + a) * Q_TILE, Q_TILE), Q_TILE)
        s_t = lax.dot_general(k_t, q_ref[0, rows, q_cols[c]], NT_DIMS, preferred_element_type=F32)
    ch = a * _n_softmax(q_cols) + c
    s_ref[slot, ch, :n_keys] = s_t
    mx_ref[slot, ch] = jnp.max(s_t, axis=0, keepdims=True)


def _flash_scores(q_ref, q_cols, k_ref, s_ref, mx_ref, sup, step, plan, slot):
    for a, _, n_keys in plan:
        for c in range(_n_softmax(q_cols)):
            _score_chain(q_ref, q_cols, k_ref, s_ref, mx_ref, sup, step, a, c, slot, n_keys)


EVERYONE = [(a, None, K_STEP) for a in range(Q_SUB)]


def _flash_super_tile(q_ref, q_cols, k_ref, vt_ref, s_ref, mx_ref, m_ref, acc_ref, sup, next_sup, finalize):
    n = _n_softmax(q_cols)
    m_ref[...] = jnp.full(m_ref.shape, NEG_BIG, F32)
    acc_ref[...] = jnp.zeros(acc_ref.shape, F32)
    first_step = sup * STEPS_PER_SUPER
    ones = jnp.ones((ONES_ROWS, K_STEP), BF16)

    def update(step, plan, slot, after_chain=None, before_chain=None):
        vt_aug = jnp.concatenate([vt_ref[0, 0, step], ones], axis=0)
        for a, offset, n_keys in plan:
            for c in range(n):
                ch = a * n + c
                if before_chain is not None:
                    before_chain(a, c)
                if offset is None:
                    s_t, mx = s_ref[slot, ch], mx_ref[slot, ch]
                else:
                    s_t = jnp.where(_causal_keep(offset, n_keys), s_ref[slot, ch, :n_keys], NEG_BIG)
                    mx = jnp.max(s_t, axis=0, keepdims=True)
                m_old = m_ref[ch]
                m_new = jnp.maximum(m_old, mx)
                p = jnp.exp2(s_t - m_new).astype(BF16)
                acc_ref[ch] = jnp.exp2(m_old - m_new) * acc_ref[ch] + _dot(vt_aug[:, :n_keys], p)
                m_ref[ch] = m_new
                if after_chain is not None:
                    after_chain(a, c)

    def pair_body(jj, carry):
        step = 2 * jj
        update(step, EVERYONE, 0, before_chain=functools.partial(
            _score_chain, q_ref, q_cols, k_ref, s_ref, mx_ref, sup, step + 1, slot=1))
        update(step + 1, EVERYONE, 1, before_chain=functools.partial(
            _score_chain, q_ref, q_cols, k_ref, s_ref, mx_ref, sup, step + 2, slot=0))
        return carry

    lax.fori_loop(0, sup * (STEPS_PER_SUPER // 2), pair_body, 0)
    for e in range(STEPS_PER_SUPER):
        plan = _diag_plan(e)[::-1]
        ahead = None
        if e + 1 < STEPS_PER_SUPER:
            keys_next = {a: n_keys for a, _, n_keys in _diag_plan(e + 1)}

            def ahead(a, c, e=e, keys_next=keys_next):
                if a in keys_next:
                    _score_chain(q_ref, q_cols, k_ref, s_ref, mx_ref, sup, first_step + e + 1, a, c, (e + 1) % 2,
                                 keys_next[a])
        refill = None
        if next_sup is not None and e == STEPS_PER_SUPER - 2:
            assert e % 2 == 0
            refill = functools.partial(_score_chain, q_ref, q_cols, k_ref, s_ref, mx_ref, next_sup, 0, slot=0)
            for a in range(Q_SUB):
                if a not in [b for b, _, _ in plan]:
                    for c in range(n):
                        refill(a, c)
        update(first_step + e, plan, e % 2, refill, ahead)
        for a in range(Q_SUB):
            if (Q_TILE * a + Q_TILE - 1) // K_STEP == e:
                finalize(a)


def _run_super_tiles(super_tile, q_ref, q_cols, k_ref, s_ref, mx_ref, n_super):
    order = list(range(n_super))[::-1]
    first = order[0]
    _flash_scores(q_ref, q_cols, k_ref, s_ref, mx_ref, first, 0, EVERYONE if first > 0 else _diag_plan(0), 0)
    for sup, next_sup in zip(order, order[1:] + [None]):
        super_tile(sup, next_sup)


def _softmax_out(acc_ref, ch):
    acc = acc_ref[ch]
    return acc[:V_DIM] * (1.0 / acc[V_DIM:V_DIM + 1])


def _diff_attn_kernel(qz_ref, k_ref, vt_ref, g_ref, lq1_ref, lk1_ref, lq2_ref, lk2_ref, o_ref,
                      s_ref, mx_ref, m_ref, acc_ref, *, lambda_init):
    lam = (jnp.exp(jnp.sum(lq1_ref[...] * lk1_ref[...], keepdims=True))
           - jnp.exp(jnp.sum(lq2_ref[...] * lk2_ref[...], keepdims=True)) + lambda_init)
    g_col = g_ref[...]
    q_cols = [slice(0, LANES), slice(LANES, SLOT)]

    def super_tile(sup, next_sup):
        def finalize(a):
            o_t = _softmax_out(acc_ref, 2 * a) - lam * _softmax_out(acc_ref, 2 * a + 1)
            y_t = o_t * lax.rsqrt(jnp.mean(o_t * o_t, axis=0, keepdims=True) + SUBLN_EPS) * g_col
            rows = pl.ds(pl.multiple_of((sup * Q_SUB + a) * Q_TILE, Q_TILE), Q_TILE)
            o_ref[0, rows, :] = (y_t * (1.0 - lambda_init)).T.astype(BF16)

        _flash_super_tile(qz_ref, q_cols, k_ref, vt_ref, s_ref, mx_ref, m_ref, acc_ref, sup, next_sup, finalize)

    _run_super_tiles(super_tile, qz_ref, q_cols, k_ref, s_ref, mx_ref, o_ref.shape[1] // (Q_TILE * Q_SUB))


def _mla_attn_kernel(q_ref, kn_ref, kr_ref, vt_ref, o_ref, s_ref, mx_ref, m_ref, acc_ref):
    q_cols = None
    k_ref = (kn_ref, kr_ref)

    def super_tile(sup, next_sup):
        def finalize(a):
            rows = pl.ds(pl.multiple_of((sup * Q_SUB + a) * Q_TILE, Q_TILE), Q_TILE)
            o_ref[0, rows, :] = _softmax_out(acc_ref, a).T.astype(BF16)

        _flash_super_tile(q_ref, q_cols, k_ref, vt_ref, s_ref, mx_ref, m_ref, acc_ref, sup, next_sup, finalize)

    _run_super_tiles(super_tile, q_ref, q_cols, k_ref, s_ref, mx_ref, o_ref.shape[1] // (Q_TILE * Q_SUB))


def _attn_call(kernel, n_softmax, q, k, k_width, vt, extra, name, k_shared=None):
    b, s, _ = k.shape
    chains = n_softmax * Q_SUB
    small = lambda bi, hi: (0, 0)
    if q.ndim == 3:
        q_spec = pl.BlockSpec((1, s, SLOT), lambda bi, hi: (bi, 0, hi))
    else:
        q_spec = pl.BlockSpec((1, 1) + q.shape[2:], lambda bi, hi: (bi, hi, 0, 0, 0))
    in_specs = [
        q_spec,
        pl.BlockSpec((1, s, k_width), lambda bi, hi: (bi, 0, hi)),
    ] + ([] if k_shared is None else [pl.BlockSpec((1, s, k_shared.shape[-1]), lambda bi, hi: (bi, 0, 0))]) + [
        pl.BlockSpec((1, 1, s // K_STEP, V_DIM, K_STEP), lambda bi, hi: (bi, hi, 0, 0, 0)),
    ] + [pl.BlockSpec(e.shape, small) for e in extra]
    return pl.pallas_call(
        kernel,
        grid=(b, HEADS),
        in_specs=in_specs,
        out_specs=pl.BlockSpec((1, s, V_DIM), lambda bi, hi: (bi, 0, hi)),
        out_shape=jax.ShapeDtypeStruct((b, s, HEADS * V_DIM), BF16),
        scratch_shapes=[
            pltpu.VMEM((2, chains, K_STEP, Q_TILE), F32),
            pltpu.VMEM((2, chains, 1, Q_TILE), F32),
            pltpu.VMEM((chains, 1, Q_TILE), F32),
            pltpu.VMEM((chains, V_DIM + ONES_ROWS, Q_TILE), F32),
        ],
        compiler_params=pltpu.CompilerParams(dimension_semantics=("arbitrary", "arbitrary"),
                                             vmem_limit_bytes=VMEM_LIMIT),
        name=name,
    )(q, k, *([] if k_shared is None else [k_shared]), vt, *extra)


def _post_kernel(h_ref, o_ref, wo_ref, g_ref, wup_ref, wdn_ref, gf_ref, out_ref, *, final_norm):
    h1 = h_ref[...] + _dot(o_ref[...], wo_ref[...])
    hn = _rms(h1, g_ref[...], EPS).astype(BF16)
    acc = h1
    for f in range(D_FF // FF_CHUNK):
        cols = slice(f * FF_CHUNK, (f + 1) * FF_CHUNK)
        u = jnp.maximum(_dot(hn, wup_ref[:, cols]), 0.0)
        acc = acc + _dot((u * u).astype(BF16), wdn_ref[cols, :])
    if final_norm:
        acc = _rms(acc, gf_ref[...], EPS)
    out_ref[...] = acc


def _post(h2, o2, w_o, g_mlp, w_up, w_down, g_final, final_norm, name):
    t = h2.shape[0]
    tm = MLP_TILE
    const = lambda i: (0, 0)
    row = lambda i: (i, 0)
    resident = functools.partial(pl.BlockSpec, index_map=const, pipeline_mode=pl.Buffered(1))
    return pl.pallas_call(
        functools.partial(_post_kernel, final_norm=final_norm),
        grid=(t // tm,),
        in_specs=[
            pl.BlockSpec((tm, D_MODEL), row),
            pl.BlockSpec((tm, D_MODEL), row),
            resident((D_MODEL, D_MODEL)),
            pl.BlockSpec((1, D_MODEL), const),
            resident((D_MODEL, D_FF)),
            resident((D_FF, D_MODEL)),
            pl.BlockSpec((1, D_MODEL), const),
        ],
        out_specs=pl.BlockSpec((tm, D_MODEL), row),
        out_shape=jax.ShapeDtypeStruct((t, D_MODEL), F32),
        compiler_params=pltpu.CompilerParams(dimension_semantics=("arbitrary",), vmem_limit_bytes=VMEM_LIMIT),
        name=name,
    )(h2, o2, w_o, g_mlp, w_up, w_down, g_final)


def _proj_b_kernel(h_ref, gkv_ref, wdkv_ref, gc_ref, wukv_ref, gq_ref, wdq_ref, gcq_ref, wuqt_ref,
                   cos_ref, sin_ref, cost_ref, sint_ref, q_ref, k_ref, kr_ref, vt_ref):
    h = h_ref[...]
    cos, sin = cos_ref[...], sin_ref[...]
    ckv = _dot(_rms(h, gkv_ref[...], EPS).astype(BF16), wdkv_ref[...])
    cq = _dot(_rms(h, gq_ref[...], EPS).astype(BF16), wdq_ref[...])
    c = _rms(ckv[:, :KV_LORA], gc_ref[...], EPS).astype(BF16)
    kr_ref[...] = _rope_slab(ckv[:, KV_LORA:], cos, sin).astype(BF16)
    kv = _dot(c, wukv_ref[...])
    kv_head = MLA_NOPE + V_DIM
    for hd in range(HEADS):
        k_ref[:, hd * MLA_NOPE:(hd + 1) * MLA_NOPE] = kv[:, hd * kv_head:hd * kv_head + MLA_NOPE].astype(BF16)
    for hd in range(HEADS):
        for j in range(TOKEN_TILE // K_STEP):
            v = kv[j * K_STEP:(j + 1) * K_STEP, hd * kv_head + MLA_NOPE:(hd + 1) * kv_head]
            vt_ref[0, hd, j] = v.T.astype(BF16)
    q_t = _dot(wuqt_ref[...], _rms(cq, gcq_ref[...], EPS).T.astype(BF16))
    cos_t, sin_t = cost_ref[...], sint_ref[...]
    scale = (MLA_NOPE + MLA_ROPE) ** -0.5 * LOG2_E
    half = MLA_ROPE // 2
    pad = jnp.zeros((SLOT - MLA_NOPE - MLA_ROPE, q_t.shape[1]), F32)
    for hd in range(HEADS):
        base = hd * (MLA_NOPE + MLA_ROPE)
        x1 = q_t[base + MLA_NOPE:base + MLA_NOPE + half]
        x2 = q_t[base + MLA_NOPE + half:base + MLA_NOPE + MLA_ROPE]
        tile = jnp.concatenate([q_t[base:base + MLA_NOPE], x1 * cos_t - x2 * sin_t, x2 * cos_t + x1 * sin_t, pad],
                               axis=0) * scale
        for j in range(q_t.shape[1] // Q_TILE):
            q_ref[0, hd, j] = tile[:, j * Q_TILE:(j + 1) * Q_TILE].astype(BF16)


def _proj_b(h2, g_kv_in, w_dkv, g_c, w_ukv, g_q, w_dq, g_cq, w_uq_t, cos, sin, cos_t, sin_t, seq):
    t = h2.shape[0]
    tm = TOKEN_TILE
    pos_blocks = seq // tm
    const = lambda i: (0, 0)
    row = lambda i: (i, 0)
    full = lambda a: pl.BlockSpec(a.shape, const)
    return pl.pallas_call(
        _proj_b_kernel,
        grid=(t // tm,),
        in_specs=[
            pl.BlockSpec((tm, D_MODEL), row),
            full(g_kv_in), full(w_dkv), full(g_c), full(w_ukv),
            full(g_q), full(w_dq), full(g_cq), full(w_uq_t),
            pl.BlockSpec((tm, LANES), lambda i: (i % pos_blocks, 0)),
            pl.BlockSpec((tm, LANES), lambda i: (i % pos_blocks, 0)),
            pl.BlockSpec((MLA_ROPE // 2, tm), lambda i: (0, i % pos_blocks)),
            pl.BlockSpec((MLA_ROPE // 2, tm), lambda i: (0, i % pos_blocks)),
        ],
        out_specs=[
            pl.BlockSpec((1, HEADS, tm // Q_TILE, SLOT, Q_TILE), lambda i: (i // pos_blocks, 0, i % pos_blocks, 0, 0)),
            pl.BlockSpec((tm, HEADS * MLA_NOPE), row),
            pl.BlockSpec((tm, LANES), row),
            _values_t_spec(pos_blocks),
        ],
        out_shape=[
            jax.ShapeDtypeStruct((t // seq, HEADS, seq // Q_TILE, SLOT, Q_TILE), BF16),
            jax.ShapeDtypeStruct((t, HEADS * MLA_NOPE), BF16),
            jax.ShapeDtypeStruct((t, LANES), BF16),
            _values_t_shape(t, seq),
        ],
        compiler_params=pltpu.CompilerParams(dimension_semantics=("arbitrary",), vmem_limit_bytes=VMEM_LIMIT),
        name="proj_b",
    )(h2, g_kv_in, w_dkv, g_c, w_ukv, g_q, w_dq, g_cq, w_uq_t, cos, sin, cos_t, sin_t)


def _rope_tables(seq):
    pos = jnp.arange(seq, dtype=F32)
    inv_freq = ROPE_THETA ** (-jnp.arange(0, HEAD_DIM, 2, dtype=F32) / HEAD_DIM)
    ang = pos[:, None] * inv_freq[None, :]
    cos, sin = jnp.cos(ang), jnp.sin(ang)
    return jnp.tile(cos, (1, 4)), jnp.tile(jnp.concatenate([-sin, sin], axis=1), (1, 2)), cos.T, sin.T


def _pad_cols(w, width):
    return jnp.pad(w, ((0, 0), (0, width - w.shape[1])))


def kernel(x, attn_norm_g, w_qkv_a, lambda_q1, lambda_k1, lambda_q2, lambda_k2, subln_g, w_o_a, kv_in_norm_g, w_dkv, kv_norm_g, w_ukv, w_dq, q_norm_g, w_uq, w_o_b, mlp_norm_g, w_up, w_down, final_norm_g):
    b, s, _ = x.shape
    t = b * s
    cos, sin, cos_t, sin_t = _rope_tables(s)
    row = lambda g: g.reshape(1, -1)
    h = x.reshape(t, D_MODEL)

    lambda_init = 0.8 - 0.6 * math.exp(-0.3 * 0)
    qz, k, vt = _proj_a(h, row(attn_norm_g[0]), w_qkv_a[0].astype(BF16), cos, sin, s)
    lam_rows = [row(p[0]) for p in (lambda_q1, lambda_k1, lambda_q2, lambda_k2)]
    o = _attn_call(functools.partial(_diff_attn_kernel, lambda_init=lambda_init), 2,
                   qz.reshape(b, s, HEADS * SLOT), k.reshape(b, s, D_MODEL), LANES,
                   vt, [subln_g[0].reshape(V_DIM, 1)] + lam_rows, "diff_attn")
    h = _post(h, o.reshape(t, D_MODEL), w_o_a[0].astype(BF16), row(mlp_norm_g[0]),
              w_up[0].astype(BF16), w_down[0].astype(BF16), row(final_norm_g), False, "post_a")

    q, kk, kr, vt = _proj_b(h, row(kv_in_norm_g), _pad_cols(w_dkv, KV_LORA + LANES).astype(BF16), row(kv_norm_g),
                       w_ukv.astype(BF16), row(attn_norm_g[1]), w_dq[0].astype(BF16),
                       row(q_norm_g[0]), w_uq[0].T.astype(BF16), cos, sin, cos_t, sin_t, s)
    o = _attn_call(_mla_attn_kernel, 1, q, kk.reshape(b, s, HEADS * MLA_NOPE), MLA_NOPE,
                   vt, [], "mla_attn", k_shared=kr.reshape(b, s, LANES))
    h = _post(h, o.reshape(t, D_MODEL), w_o_b[0].astype(BF16), row(mlp_norm_g[1]),
              w_up[1].astype(BF16), w_down[1].astype(BF16), row(final_norm_g), True, "post_b")
    return h.reshape(b, s, D_MODEL)
```

```python
import functools
import math

import jax
import jax.numpy as jnp
from jax import lax
from jax.experimental import pallas as pl
from jax.experimental.pallas import tpu as pltpu

D_MODEL = 1024
HEADS = 8
HEAD_DIM = 64
V_DIM = 128
MLA_NOPE = 128
MLA_ROPE = 64
Q_LORA = 384
KV_LORA = 256
D_FF = 4 * D_MODEL
ROPE_THETA = 10000.0
EPS = 1e-6
SUBLN_EPS = 1e-5

LANES = 128
SLOT = 2 * LANES
TOKEN_TILE = 1024
MLP_TILE = 1024
Q_TILE = 256
K_STEP = 512
Q_SUB = 8
STEPS_PER_SUPER = Q_SUB * Q_TILE // K_STEP
ONES_ROWS = 16
FF_CHUNK = 1024
NEG_BIG = -1e30
LOG2_E = math.log2(math.e)
VMEM_LIMIT = 56 * 1024 * 1024

BF16 = jnp.bfloat16
F32 = jnp.float32
NT_DIMS = (((1,), (1,)), ((), ()))


def _rms(x, g, eps):
    return x * lax.rsqrt(jnp.mean(x * x, axis=-1, keepdims=True) + eps) * g


def _rope_slab(x, cos, sin_signed):
    lane = lax.broadcasted_iota(jnp.int32, x.shape, 1)
    first = (lane % HEAD_DIM) < (HEAD_DIM // 2)
    swapped = jnp.where(first, pltpu.roll(x, LANES - HEAD_DIM // 2, 1), pltpu.roll(x, HEAD_DIM // 2, 1))
    return x * cos + swapped * sin_signed


def _dot(a, b):
    return jnp.dot(a, b, preferred_element_type=F32)


def _values_t_spec(steps_per_seq):
    return pl.BlockSpec((1, HEADS, TOKEN_TILE // K_STEP, V_DIM, K_STEP),
                        lambda i: (i // steps_per_seq, 0, i % steps_per_seq, 0, 0))


def _values_t_shape(tokens, seq):
    return jax.ShapeDtypeStruct((tokens // seq, HEADS, seq // K_STEP, V_DIM, K_STEP), BF16)


def _proj_a_kernel(x_ref, g_ref, w_ref, cos_ref, sin_ref, qz_ref, k_ref, vt_ref):
    hn = _rms(x_ref[...], g_ref[...], EPS).astype(BF16)
    qkv = _dot(hn, w_ref[...])
    cos, sin = cos_ref[...], sin_ref[...]
    lane = lax.broadcasted_iota(jnp.int32, (x_ref.shape[0], LANES), 1)
    comp1 = lane < HEAD_DIM
    scale = HEAD_DIM ** -0.5 * LOG2_E
    for h in range(HEADS):
        q = _rope_slab(qkv[:, h * LANES:(h + 1) * LANES], cos, sin) * scale
        qz_ref[:, h * SLOT:h * SLOT + LANES] = jnp.where(comp1, q, 0.0).astype(BF16)
        qz_ref[:, h * SLOT + LANES:(h + 1) * SLOT] = jnp.where(comp1, 0.0, q).astype(BF16)
        k = _rope_slab(qkv[:, D_MODEL + h * LANES:D_MODEL + (h + 1) * LANES], cos, sin)
        k_ref[:, h * LANES:(h + 1) * LANES] = k.astype(BF16)
        for j in range(TOKEN_TILE // K_STEP):
            v = qkv[j * K_STEP:(j + 1) * K_STEP, 2 * D_MODEL + h * V_DIM:2 * D_MODEL + (h + 1) * V_DIM]
            vt_ref[0, h, j] = v.T.astype(BF16)


def _proj_a(x2, g, w_qkv, cos, sin, seq):
    t = x2.shape[0]
    tm = TOKEN_TILE
    pos_blocks = seq // tm
    const = lambda i: (0, 0)
    return pl.pallas_call(
        _proj_a_kernel,
        grid=(t // tm,),
        in_specs=[
            pl.BlockSpec((tm, D_MODEL), lambda i: (i, 0)),
            pl.BlockSpec((1, D_MODEL), const),
            pl.BlockSpec((D_MODEL, 3 * D_MODEL), const),
            pl.BlockSpec((tm, LANES), lambda i: (i % pos_blocks, 0)),
            pl.BlockSpec((tm, LANES), lambda i: (i % pos_blocks, 0)),
        ],
        out_specs=[
            pl.BlockSpec((tm, HEADS * SLOT), lambda i: (i, 0)),
            pl.BlockSpec((tm, D_MODEL), lambda i: (i, 0)),
            _values_t_spec(pos_blocks),
        ],
        out_shape=[
            jax.ShapeDtypeStruct((t, HEADS * SLOT), BF16),
            jax.ShapeDtypeStruct((t, D_MODEL), BF16),
            _values_t_shape(t, seq),
        ],
        compiler_params=pltpu.CompilerParams(dimension_semantics=("arbitrary",), vmem_limit_bytes=VMEM_LIMIT),
        name="proj_a",
    )(x2, g, w_qkv, cos, sin)


def _causal_keep(offset, n_keys):
    key = lax.broadcasted_iota(jnp.int32, (n_keys, Q_TILE), 0)
    qry = lax.broadcasted_iota(jnp.int32, (n_keys, Q_TILE), 1)
    return key <= qry + offset


def _diag_plan(e):
    plan = []
    for a in range(Q_SUB):
        offset = Q_TILE * a - K_STEP * e
        if offset + Q_TILE > 0:
            plan.append((a, None if offset >= K_STEP - 1 else offset, min(K_STEP, offset + Q_TILE)))
    return plan


def _load_keys(k_ref, rows):
    if isinstance(k_ref, tuple):
        return jnp.concatenate([r[0, rows, :] for r in k_ref], axis=1)
    return k_ref[0, rows, :]


def _n_softmax(q_cols):
    return 1 if q_cols is None else len(q_cols)


def _score_chain(q_ref, q_cols, k_ref, s_ref, mx_ref, sup, step, a, c, slot, n_keys=K_STEP):
    k_t = _load_keys(k_ref, pl.ds(pl.multiple_of(step * K_STEP, K_STEP), n_keys))
    if q_cols is None:
        s_t = _dot(k_t, q_ref[0, 0, sup * Q_SUB + a])
    else:
        rows = pl.ds(pl.multiple_of((sup * Q_SUB + a) * Q_TILE, Q_TILE), Q_TILE)
        s_t = lax.dot_general(k_t, q_ref[0, rows, q_cols[c]], NT_DIMS, preferred_element_type=F32)
    ch = a * _n_softmax(q_cols) + c
    s_ref[slot, ch, :n_keys] = s_t
    mx_ref[slot, ch] = jnp.max(s_t, axis=0, keepdims=True)


def _flash_scores(q_ref, q_cols, k_ref, s_ref, mx_ref, sup, step, plan, slot):
    for a, _, n_keys in plan:
        for c in range(_n_softmax(q_cols)):
            _score_chain(q_ref, q_cols, k_ref, s_ref, mx_ref, sup, step, a, c, slot, n_keys)


EVERYONE = [(a, None, K_STEP) for a in range(Q_SUB)]


def _flash_super_tile(q_ref, q_cols, k_ref, vt_ref, s_ref, mx_ref, m_ref, acc_ref, sup, next_sup, finalize):
    n = _n_softmax(q_cols)
    m_ref[...] = jnp.full(m_ref.shape, NEG_BIG, F32)
    acc_ref[...] = jnp.zeros(acc_ref.shape, F32)
    first_step = sup * STEPS_PER_SUPER
    ones = jnp.ones((ONES_ROWS, K_STEP), BF16)

    def update(step, plan, slot, after_chain=None, before_chain=None):
        vt_aug = jnp.concatenate([vt_ref[0, 0, step], ones], axis=0)
        for a, offset, n_keys in plan:
            for c in range(n):
                ch = a * n + c
                if before_chain is not None:
                    before_chain(a, c)
                if offset is None:
                    s_t, mx = s_ref[slot, ch], mx_ref[slot, ch]
                else:
                    s_t = jnp.where(_causal_keep(offset, n_keys), s_ref[slot, ch, :n_keys], NEG_BIG)
                    mx = jnp.max(s_t, axis=0, keepdims=True)
                m_old = m_ref[ch]
                m_new = jnp.maximum(m_old, mx)
                p = jnp.exp2(s_t - m_new).astype(BF16)
                acc_ref[ch] = jnp.exp2(m_old - m_new) * acc_ref[ch] + _dot(vt_aug[:, :n_keys], p)
                m_ref[ch] = m_new
                if after_chain is not None:
                    after_chain(a, c)

    def pair_body(jj, carry):
        step = 2 * jj
        update(step, EVERYONE, 0, before_chain=functools.partial(
            _score_chain, q_ref, q_cols, k_ref, s_ref, mx_ref, sup, step + 1, slot=1))
        update(step + 1, EVERYONE, 1, before_chain=functools.partial(
            _score_chain, q_ref, q_cols, k_ref, s_ref, mx_ref, sup, step + 2, slot=0))
        return carry

    lax.fori_loop(0, sup * (STEPS_PER_SUPER // 2), pair_body, 0)
    for e in range(STEPS_PER_SUPER):
        plan = _diag_plan(e)[::-1]
        ahead = None
        if e + 1 < STEPS_PER_SUPER:
            keys_next = {a: n_keys for a, _, n_keys in _diag_plan(e + 1)}

            def ahead(a, c, e=e, keys_next=keys_next):
                if a in keys_next:
                    _score_chain(q_ref, q_cols, k_ref, s_ref, mx_ref, sup, first_step + e + 1, a, c, (e + 1) % 2,
                                 keys_next[a])
        refill = None
        if next_sup is not None and e == STEPS_PER_SUPER - 2:
            assert e % 2 == 0
            refill = functools.partial(_score_chain, q_ref, q_cols, k_ref, s_ref, mx_ref, next_sup, 0, slot=0)
            for a in range(Q_SUB):
                if a not in [b for b, _, _ in plan]:
                    for c in range(n):
                        refill(a, c)
        update(first_step + e, plan, e % 2, refill, ahead)
        for a in range(Q_SUB):
            if (Q_TILE * a + Q_TILE - 1) // K_STEP == e:
                finalize(a)


def _run_super_tiles(super_tile, q_ref, q_cols, k_ref, s_ref, mx_ref, n_super):
    order = list(range(n_super))[::-1]
    first = order[0]
    _flash_scores(q_ref, q_cols, k_ref, s_ref, mx_ref, first, 0, EVERYONE if first > 0 else _diag_plan(0), 0)
    for sup, next_sup in zip(order, order[1:] + [None]):
        super_tile(sup, next_sup)


def _softmax_out(acc_ref, ch):
    acc = acc_ref[ch]
    return acc[:V_DIM] * (1.0 / acc[V_DIM:V_DIM + 1])


def _diff_attn_kernel(qz_ref, k_ref, vt_ref, g_ref, lq1_ref, lk1_ref, lq2_ref, lk2_ref, o_ref,
                      s_ref, mx_ref, m_ref, acc_ref, *, lambda_init):
    lam = (jnp.exp(jnp.sum(lq1_ref[...] * lk1_ref[...], keepdims=True))
           - jnp.exp(jnp.sum(lq2_ref[...] * lk2_ref[...], keepdims=True)) + lambda_init)
    g_col = g_ref[...]
    q_cols = [slice(0, LANES), slice(LANES, SLOT)]

    def super_tile(sup, next_sup):
        def finalize(a):
            o_t = _softmax_out(acc_ref, 2 * a) - lam * _softmax_out(acc_ref, 2 * a + 1)
            y_t = o_t * lax.rsqrt(jnp.mean(o_t * o_t, axis=0, keepdims=True) + SUBLN_EPS) * g_col
            rows = pl.ds(pl.multiple_of((sup * Q_SUB + a) * Q_TILE, Q_TILE), Q_TILE)
            o_ref[0, rows, :] = (y_t * (1.0 - lambda_init)).T.astype(BF16)

        _flash_super_tile(qz_ref, q_cols, k_ref, vt_ref, s_ref, mx_ref, m_ref, acc_ref, sup, next_sup, finalize)

    _run_super_tiles(super_tile, qz_ref, q_cols, k_ref, s_ref, mx_ref, o_ref.shape[1] // (Q_TILE * Q_SUB))


def _mla_attn_kernel(q_ref, kn_ref, kr_ref, vt_ref, o_ref, s_ref, mx_ref, m_ref, acc_ref):
    q_cols = None
    k_ref = (kn_ref, kr_ref)

    def super_tile(sup, next_sup):
        def finalize(a):
            rows = pl.ds(pl.multiple_of((sup * Q_SUB + a) * Q_TILE, Q_TILE), Q_TILE)
            o_ref[0, rows, :] = _softmax_out(acc_ref, a).T.astype(BF16)

        _flash_super_tile(q_ref, q_cols, k_ref, vt_ref, s_ref, mx_ref, m_ref, acc_ref, sup, next_sup, finalize)

    _run_super_tiles(super_tile, q_ref, q_cols, k_ref, s_ref, mx_ref, o_ref.shape[1] // (Q_TILE * Q_SUB))


def _attn_call(kernel, n_softmax, q, k, k_width, vt, extra, name, k_shared=None):
    b, s, _ = k.shape
    chains = n_softmax * Q_SUB
    small = lambda bi, hi: (0, 0)
    if q.ndim == 3:
        q_spec = pl.BlockSpec((1, s, SLOT), lambda bi, hi: (bi, 0, hi))
    else:
        q_spec = pl.BlockSpec((1, 1) + q.shape[2:], lambda bi, hi: (bi, hi, 0, 0, 0))
    in_specs = [
        q_spec,
        pl.BlockSpec((1, s, k_width), lambda bi, hi: (bi, 0, hi)),
    ] + ([] if k_shared is None else [pl.BlockSpec((1, s, k_shared.shape[-1]), lambda bi, hi: (bi, 0, 0))]) + [
        pl.BlockSpec((1, 1, s // K_STEP, V_DIM, K_STEP), lambda bi, hi: (bi, hi, 0, 0, 0)),
    ] + [pl.BlockSpec(e.shape, small) for e in extra]
    return pl.pallas_call(
        kernel,
        grid=(b, HEADS),
        in_specs=in_specs,
        out_specs=pl.BlockSpec((1, s, V_DIM), lambda bi, hi: (bi, 0, hi)),
        out_shape=jax.ShapeDtypeStruct((b, s, HEADS * V_DIM), BF16),
        scratch_shapes=[
            pltpu.VMEM((2, chains, K_STEP, Q_TILE), F32),
            pltpu.VMEM((2, chains, 1, Q_TILE), F32),
            pltpu.VMEM((chains, 1, Q_TILE), F32),
            pltpu.VMEM((chains, V_DIM + ONES_ROWS, Q_TILE), F32),
        ],
        compiler_params=pltpu.CompilerParams(dimension_semantics=("arbitrary", "arbitrary"),
                                             vmem_limit_bytes=VMEM_LIMIT),
        name=name,
    )(q, k, *([] if k_shared is None else [k_shared]), vt, *extra)


def _post_kernel(h_ref, o_ref, wo_ref, g_ref, wup_ref, wdn_ref, gf_ref, out_ref, *, final_norm):
    h1 = h_ref[...] + _dot(o_ref[...], wo_ref[...])
    hn = _rms(h1, g_ref[...], EPS).astype(BF16)
    acc = h1
    for f in range(D_FF // FF_CHUNK):
        cols = slice(f * FF_CHUNK, (f + 1) * FF_CHUNK)
        u = jnp.maximum(_dot(hn, wup_ref[:, cols]), 0.0)
        acc = acc + _dot((u * u).astype(BF16), wdn_ref[cols, :])
    if final_norm:
        acc = _rms(acc, gf_ref[...], EPS)
    out_ref[...] = acc


def _post(h2, o2, w_o, g_mlp, w_up, w_down, layer, g_final, final_norm, name):
    t = h2.shape[0]
    tm = MLP_TILE
    const = lambda i: (0, 0)
    row = lambda i: (i, 0)
    resident = functools.partial(pl.BlockSpec, index_map=const, pipeline_mode=pl.Buffered(1))
    of_layer = functools.partial(pl.BlockSpec, index_map=lambda i: (layer, 0, 0), pipeline_mode=pl.Buffered(1))
    return pl.pallas_call(
        functools.partial(_post_kernel, final_norm=final_norm),
        grid=(t // tm,),
        in_specs=[
            pl.BlockSpec((tm, D_MODEL), row),
            pl.BlockSpec((tm, D_MODEL), row),
            resident((D_MODEL, D_MODEL)),
            pl.BlockSpec((1, D_MODEL), const),
            of_layer((None, D_MODEL, D_FF)),
            of_layer((None, D_FF, D_MODEL)),
            pl.BlockSpec((1, D_MODEL), const),
        ],
        out_specs=pl.BlockSpec((tm, D_MODEL), row),
        out_shape=jax.ShapeDtypeStruct((t, D_MODEL), F32),
        compiler_params=pltpu.CompilerParams(dimension_semantics=("arbitrary",), vmem_limit_bytes=VMEM_LIMIT),
        name=name,
    )(h2, o2, w_o, g_mlp, w_up, w_down, g_final)


def _proj_b_kernel(h_ref, gkv_ref, wdkv_ref, gc_ref, wukv_ref, gq_ref, wdq_ref, gcq_ref, wuqt_ref,
                   cos_ref, sin_ref, cost_ref, sint_ref, q_ref, k_ref, kr_ref, vt_ref):
    h = h_ref[...]
    cos, sin = cos_ref[...], sin_ref[...]
    ckv = _dot(_rms(h, gkv_ref[...], EPS).astype(BF16), wdkv_ref[...])
    cq = _dot(_rms(h, gq_ref[...], EPS).astype(BF16), wdq_ref[...])
    c = _rms(ckv[:, :KV_LORA], gc_ref[...], EPS).astype(BF16)
    kr_ref[...] = _rope_slab(ckv[:, KV_LORA:], cos, sin).astype(BF16)
    kv = _dot(c, wukv_ref[...])
    kv_head = MLA_NOPE + V_DIM
    for hd in range(HEADS):
        k_ref[:, hd * MLA_NOPE:(hd + 1) * MLA_NOPE] = kv[:, hd * kv_head:hd * kv_head + MLA_NOPE].astype(BF16)
    for hd in range(HEADS):
        for j in range(TOKEN_TILE // K_STEP):
            v = kv[j * K_STEP:(j + 1) * K_STEP, hd * kv_head + MLA_NOPE:(hd + 1) * kv_head]
            vt_ref[0, hd, j] = v.T.astype(BF16)
    q_t = _dot(wuqt_ref[...], _rms(cq, gcq_ref[...], EPS).T.astype(BF16))
    cos_t, sin_t = cost_ref[...], sint_ref[...]
    scale = (MLA_NOPE + MLA_ROPE) ** -0.5 * LOG2_E
    half = MLA_ROPE // 2
    pad = jnp.zeros((SLOT - MLA_NOPE - MLA_ROPE, q_t.shape[1]), F32)
    for hd in range(HEADS):
        base = hd * (MLA_NOPE + MLA_ROPE)
        x1 = q_t[base + MLA_NOPE:base + MLA_NOPE + half]
        x2 = q_t[base + MLA_NOPE + half:base + MLA_NOPE + MLA_ROPE]
        tile = jnp.concatenate([q_t[base:base + MLA_NOPE], x1 * cos_t - x2 * sin_t, x2 * cos_t + x1 * sin_t, pad],
                               axis=0) * scale
        for j in range(q_t.shape[1] // Q_TILE):
            q_ref[0, hd, j] = tile[:, j * Q_TILE:(j + 1) * Q_TILE].astype(BF16)


def _proj_b(h2, g_kv_in, w_dkv, g_c, w_ukv, g_q, w_dq, g_cq, w_uq_t, cos, sin, cos_t, sin_t, seq):
    t = h2.shape[0]
    tm = TOKEN_TILE
    pos_blocks = seq // tm
    const = lambda i: (0, 0)
    row = lambda i: (i, 0)
    full = lambda a: pl.BlockSpec(a.shape, const)
    return pl.pallas_call(
        _proj_b_kernel,
        grid=(t // tm,),
        in_specs=[
            pl.BlockSpec((tm, D_MODEL), row),
            full(g_kv_in), full(w_dkv), full(g_c), full(w_ukv),
            full(g_q), full(w_dq), full(g_cq), full(w_uq_t),
            pl.BlockSpec((tm, LANES), lambda i: (i % pos_blocks, 0)),
            pl.BlockSpec((tm, LANES), lambda i: (i % pos_blocks, 0)),
            pl.BlockSpec((MLA_ROPE // 2, tm), lambda i: (0, i % pos_blocks)),
            pl.BlockSpec((MLA_ROPE // 2, tm), lambda i: (0, i % pos_blocks)),
        ],
        out_specs=[
            pl.BlockSpec((1, HEADS, tm // Q_TILE, SLOT, Q_TILE), lambda i: (i // pos_blocks, 0, i % pos_blocks, 0, 0)),
            pl.BlockSpec((tm, HEADS * MLA_NOPE), row),
            pl.BlockSpec((tm, LANES), row),
            _values_t_spec(pos_blocks),
        ],
        out_shape=[
            jax.ShapeDtypeStruct((t // seq, HEADS, seq // Q_TILE, SLOT, Q_TILE), BF16),
            jax.ShapeDtypeStruct((t, HEADS * MLA_NOPE), BF16),
            jax.ShapeDtypeStruct((t, LANES), BF16),
            _values_t_shape(t, seq),
        ],
        compiler_params=pltpu.CompilerParams(dimension_semantics=("arbitrary",), vmem_limit_bytes=VMEM_LIMIT),
        name="proj_b",
    )(h2, g_kv_in, w_dkv, g_c, w_ukv, g_q, w_dq, g_cq, w_uq_t, cos, sin, cos_t, sin_t)


def _rope_tables(seq):
    pos = jnp.arange(seq, dtype=F32)
    inv_freq = ROPE_THETA ** (-jnp.arange(0, HEAD_DIM, 2, dtype=F32) / HEAD_DIM)
    ang = pos[:, None] * inv_freq[None, :]
    cos, sin = jnp.cos(ang), jnp.sin(ang)
    return jnp.tile(cos, (1, 4)), jnp.tile(jnp.concatenate([-sin, sin], axis=1), (1, 2)), cos.T, sin.T


def _pad_cols(w, width):
    return jnp.pad(w, ((0, 0), (0, width - w.shape[1])))


def kernel(x, attn_norm_g, w_qkv_a, lambda_q1, lambda_k1, lambda_q2, lambda_k2, subln_g, w_o_a, kv_in_norm_g, w_dkv, kv_norm_g, w_ukv, w_dq, q_norm_g, w_uq, w_o_b, mlp_norm_g, w_up, w_down, final_norm_g):
    b, s, _ = x.shape
    t = b * s
    cos, sin, cos_t, sin_t = _rope_tables(s)
    row = lambda g: g.reshape(1, -1)
    h = x.reshape(t, D_MODEL)
    w_up_b, w_down_b = w_up.astype(BF16), w_down.astype(BF16)

    lambda_init = 0.8 - 0.6 * math.exp(-0.3 * 0)
    qz, k, vt = _proj_a(h, row(attn_norm_g[0]), w_qkv_a[0].astype(BF16), cos, sin, s)
    lam_rows = [row(p[0]) for p in (lambda_q1, lambda_k1, lambda_q2, lambda_k2)]
    o = _attn_call(functools.partial(_diff_attn_kernel, lambda_init=lambda_init), 2,
                   qz.reshape(b, s, HEADS * SLOT), k.reshape(b, s, D_MODEL), LANES,
                   vt, [subln_g[0].reshape(V_DIM, 1)] + lam_rows, "diff_attn")
    h = _post(h, o.reshape(t, D_MODEL), w_o_a[0].astype(BF16), row(mlp_norm_g[0]),
              w_up_b, w_down_b, 0, row(final_norm_g), False, "post_a")

    q, kk, kr, vt = _proj_b(h, row(kv_in_norm_g), _pad_cols(w_dkv, KV_LORA + LANES).astype(BF16), row(kv_norm_g),
                       w_ukv.astype(BF16), row(attn_norm_g[1]), w_dq[0].astype(BF16),
                       row(q_norm_g[0]), w_uq[0].T.astype(BF16), cos, sin, cos_t, sin_t, s)
    o = _attn_call(_mla_attn_kernel, 1, q, kk.reshape(b, s, HEADS * MLA_NOPE), MLA_NOPE,
                   vt, [], "mla_attn", k_shared=kr.reshape(b, s, LANES))
    h = _post(h, o.reshape(t, D_MODEL), w_o_b[0].astype(BF16), row(mlp_norm_g[1]),
              w_up_b, w_down_b, 1, row(final_norm_g), True, "post_b")
    return h.reshape(b, s, D_MODEL)
```
